```python
import math
import jax, jax.numpy as jnp
from jax import lax
import numpy as np

D_MODEL = 2048
BATCH = 16
SEQ = 256
DEPTH = 2
DEC_BATCH = 4
DEC_SEQ = 2048
PAST_LEN = 256

GRID_W = 64
N_MIXERS = 2
N_CONV = (DEPTH + 1) // 2
N_RET = DEPTH // 2
CONV_WIDTH = 31
D_FF = 4 * D_MODEL
RET_HEADS = 8
RET_DK = D_MODEL // RET_HEADS
RET_DV = 2 * D_MODEL // RET_HEADS
RET_CHUNK = 128
ROPE_BASE = 10000.0
LN_EPS = 1e-5
GN_EPS = 1e-5
ALPHA = (2.0 * DEPTH) ** 0.25
BETA = (8.0 * DEPTH) ** -0.25

kernel_name = "hybrid_conv_retention_diffusion_step"


def layer_norm(x, g, b):
    xf = x.astype(jnp.float32)
    mu = jnp.mean(xf, -1, keepdims=True)
    var = jnp.mean(jnp.square(xf - mu), -1, keepdims=True)
    y = (xf - mu) * lax.rsqrt(var + LN_EPS)
    return (y * g.astype(jnp.float32) + b.astype(jnp.float32)).astype(x.dtype)


def conv_module(h, w_pw1, b_pw1, w_dw, b_dw, cn_g, cn_b, w_pw2, b_pw2):
    u = h @ w_pw1 + b_pw1
    a, gate = jnp.split(u, 2, axis=-1)
    u = a * jax.nn.sigmoid(gate)
    u = lax.conv_general_dilated(
        u, w_dw[:, None, :], window_strides=(1,),
        padding=((CONV_WIDTH // 2, CONV_WIDTH // 2),),
        dimension_numbers=('NWC', 'WIO', 'NWC'),
        feature_group_count=D_MODEL) + b_dw
    u = jax.nn.silu(layer_norm(u, cn_g, cn_b))
    return u @ w_pw2 + b_pw2


def rope_2d(x):
    t = x.shape[2]
    rows = t // GRID_W
    row = jnp.repeat(jnp.arange(rows, dtype=jnp.float32), GRID_W)
    col = jnp.tile(jnp.arange(GRID_W, dtype=jnp.float32), rows)
    quarter = RET_DK // 4
    inv = ROPE_BASE ** (-jnp.arange(quarter, dtype=jnp.float32) / quarter)
    ang = jnp.stack([row[:, None] * inv, col[:, None] * inv], axis=1)
    cos, sin = jnp.cos(ang), jnp.sin(ang)
    xr = x.reshape(x.shape[:3] + (2, 2, quarter))
    x1, x2 = xr[..., 0, :], xr[..., 1, :]
    out = jnp.stack([x1 * cos - x2 * sin, x2 * cos + x1 * sin], axis=-2)
    return out.reshape(x.shape)


def retention_scan(q, k, v, log_gamma, s0):
    b, h, t, _ = q.shape
    n = t // RET_CHUNK
    idx = jnp.arange(RET_CHUNK, dtype=jnp.float32)
    lg = log_gamma[:, None]
    decay_in = jnp.exp(lg * (idx + 1.0))[None, :, :, None]
    decay_out = jnp.exp(lg * (RET_CHUNK - 1.0 - idx))[None, :, :, None]
    decay_chunk = jnp.exp(log_gamma * RET_CHUNK)[None, :, None, None]
    diff = idx[:, None] - idx[None, :]
    mask = jnp.where(diff >= 0, jnp.exp(lg[..., None] * jnp.maximum(diff, 0.0)), 0.0)

    def chunks(a):
        return jnp.moveaxis(a.reshape(b, h, n, RET_CHUNK, a.shape[-1]), 2, 0)

    def step(s, qkv):
        qc, kc, vc = qkv
        scores = jnp.einsum('bhid,bhjd->bhij', qc, kc) * mask
        inner = jnp.einsum('bhij,bhje->bhie', scores, vc)
        cross = jnp.einsum('bhid,bhde->bhie', qc, s) * decay_in
        s_new = decay_chunk * s + jnp.einsum('bhjd,bhje->bhde', kc * decay_out, vc)
        return s_new, inner + cross

    s_final, out = lax.scan(step, s0, (chunks(q), chunks(k), chunks(v)))
    out = jnp.moveaxis(out, 0, 2).reshape(b, h, t, RET_DV)
    return out, s_final


def retention(h, s0, latent, w_in, log2_rate, w_o):
    b, t, _ = h.shape
    proj = h @ w_in
    q, k, v, g = jnp.split(proj, [D_MODEL, 2 * D_MODEL, 4 * D_MODEL], axis=-1)

    def heads(a, d):
        return a.reshape(b, t, RET_HEADS, d).transpose(0, 2, 1, 3).astype(jnp.float32)

    q = heads(q, RET_DK) * (RET_DK ** -0.5)
    k = heads(k, RET_DK)
    v = heads(v, RET_DV)
    if latent:
        q = rope_2d(q)
        k = rope_2d(k)
    log_gamma = jnp.log1p(-jnp.exp2(log2_rate.astype(jnp.float32)))
    s0 = s0.astype(jnp.float32)
    out_f, s_f = retention_scan(q, k, v, log_gamma[0], s0[:, 0])
    flip = lambda a: jnp.flip(a, axis=2)
    out_b, s_b = retention_scan(flip(q), flip(k), flip(v), log_gamma[1], s0[:, 1])
    out = out_f + flip(out_b)
    mu = jnp.mean(out, -1, keepdims=True)
    var = jnp.mean(jnp.square(out - mu), -1, keepdims=True)
    out = (out - mu) * lax.rsqrt(var + GN_EPS)
    out = out.transpose(0, 2, 1, 3).reshape(b, t, RET_HEADS * RET_DV).astype(h.dtype)
    y = (jax.nn.silu(g) * out) @ w_o
    return y, jnp.stack([s_f, s_b], axis=1).astype(h.dtype)


def run_trunk(x, cond, init_states, latent, w_mod, b_mod, ln1_g, ln1_b, ln2_g, ln2_b,
              w_pw1, b_pw1, w_dw, b_dw, cn_g, cn_b, w_pw2, b_pw2,
              w_ret_in, ret_log2_rate, w_ret_o, w_ff1, b_ff1, w_ff2, b_ff2):
    b = x.shape[0]
    states = []
    for i in range(DEPTH):
        mod = (jax.nn.silu(cond) @ w_mod[i] + b_mod[i])[:, None, :]
        sh1, sc1, g1, sh2, sc2, g2 = jnp.split(mod, 6, axis=-1)
        hm = x * (1 + sc1) + sh1
        j = i // N_MIXERS
        if i % N_MIXERS == 0:
            y = conv_module(hm, w_pw1[j], b_pw1[j], w_dw[j], b_dw[j], cn_g[j], cn_b[j],
                            w_pw2[j], b_pw2[j])
        else:
            if init_states is None:
                s0 = jnp.zeros((b, 2, RET_HEADS, RET_DK, RET_DV), jnp.float32)
            else:
                s0 = init_states[:, j]
            y, s = retention(hm, s0, latent, w_ret_in[j], ret_log2_rate[j], w_ret_o[j])
            states.append(s)
        x = layer_norm(ALPHA * x + g1 * y, ln1_g[i], ln1_b[i])
        hm = x * (1 + sc2) + sh2
        y = jnp.square(jax.nn.relu(hm @ w_ff1[i] + b_ff1[i])) @ w_ff2[i] + b_ff2[i]
        x = layer_norm(ALPHA * x + g2 * y, ln2_g[i], ln2_b[i])
    return x, jnp.stack(states, axis=1)


def setup_inputs(seed: int = 0) -> dict:
    key = jax.random.key(seed)
    ks = jax.random.split(key, 32)
    nrm = lambda k, shape, s: jax.random.normal(k, shape, jnp.float32) * s
    D = D_MODEL
    rate = -(5.0 + jnp.arange(RET_HEADS, dtype=jnp.float32))
    ret_log2_rate = jnp.broadcast_to(rate, (N_RET, 2, RET_HEADS)) + nrm(ks[20], (N_RET, 2, RET_HEADS), 0.1)
    return {
        "x_prompt": nrm(ks[0], (BATCH, SEQ, D), 1.0),
        "x_sample": nrm(ks[1], (DEC_BATCH, DEC_SEQ, D), 1.0),
        "state_ret": nrm(ks[2], (DEC_BATCH, N_RET, 2, RET_HEADS, RET_DK, RET_DV), 1.0),
        "c": nrm(ks[3], (DEC_BATCH, D), 1.0),
        "c_ctx": nrm(ks[4], (D,), 1.0),
        "w_mod": nrm(ks[5], (DEPTH, D, 6 * D), 0.5 * D ** -0.5),
        "b_mod": nrm(ks[6], (DEPTH, 6 * D), 0.02),
        "ln1_g": 1.0 + nrm(ks[7], (DEPTH, D), 0.02),
        "ln1_b": nrm(ks[8], (DEPTH, D), 0.02),
        "ln2_g": 1.0 + nrm(ks[9], (DEPTH, D), 0.02),
        "ln2_b": nrm(ks[10], (DEPTH, D), 0.02),
        "w_pw1": nrm(ks[11], (N_CONV, D, 2 * D), D ** -0.5),
        "b_pw1": nrm(ks[12], (N_CONV, 2 * D), 0.02),
        "w_dw": nrm(ks[13], (N_CONV, CONV_WIDTH, D), CONV_WIDTH ** -0.5),
        "b_dw": nrm(ks[14], (N_CONV, D), 0.02),
        "cn_g": 1.0 + nrm(ks[15], (N_CONV, D), 0.02),
        "cn_b": nrm(ks[16], (N_CONV, D), 0.02),
        "w_pw2": nrm(ks[17], (N_CONV, D, D), BETA * D ** -0.5),
        "b_pw2": nrm(ks[18], (N_CONV, D), 0.02),
        "w_ret_in": nrm(ks[19], (N_RET, D, 6 * D), D ** -0.5),
        "ret_log2_rate": ret_log2_rate,
        "w_ret_o": nrm(ks[21], (N_RET, 2 * D, D), BETA * (2 * D) ** -0.5),
        "w_ff1": nrm(ks[22], (DEPTH, D, D_FF), D ** -0.5),
        "b_ff1": nrm(ks[23], (DEPTH, D_FF), 0.02),
        "w_ff2": nrm(ks[24], (DEPTH, D_FF, D), BETA * D_FF ** -0.5),
        "b_ff2": nrm(ks[25], (DEPTH, D), 0.02),
    }


def reference(x_prompt, x_sample, state_ret, c, c_ctx, w_mod, b_mod, ln1_g, ln1_b, ln2_g, ln2_b,
              w_pw1, b_pw1, w_dw, b_dw, cn_g, cn_b, w_pw2, b_pw2,
              w_ret_in, ret_log2_rate, w_ret_o, w_ff1, b_ff1, w_ff2, b_ff2):
    weights = (w_mod, b_mod, ln1_g, ln1_b, ln2_g, ln2_b,
               w_pw1, b_pw1, w_dw, b_dw, cn_g, cn_b, w_pw2, b_pw2,
               w_ret_in, ret_log2_rate, w_ret_o, w_ff1, b_ff1, w_ff2, b_ff2)
    cond_ctx = jnp.broadcast_to(c_ctx, (x_prompt.shape[0], D_MODEL))
    y_prompt, new_state_ret = run_trunk(x_prompt, cond_ctx, None, False, *weights)
    y_sample, _ = run_trunk(x_sample, c, state_ret, True, *weights)
    return (y_prompt, y_sample, new_state_ret)
```

```python
import functools

import jax
import jax.numpy as jnp
from jax import lax
from jax.experimental import pallas as pl
from jax.experimental.pallas import tpu as pltpu

F32 = jnp.float32
BF16 = jnp.bfloat16

D = 2048
BATCH = 16
SEQ = 256
DEC_BATCH = 4
DEC_SEQ = 2048
GRID_W = 64
CONV_WIDTH = 31
D_FF = 4 * D
HEADS = 8
DK = D // HEADS
DV = 2 * D // HEADS
CHUNK = 128
ROPE_BASE = 10000.0
LN_EPS = 1e-5
GN_EPS = 1e-5
DEPTH = 2
ALPHA = (2.0 * DEPTH) ** 0.25

P_TOK = BATCH * SEQ
S_TOK = DEC_BATCH * DEC_SEQ
N_TOK = P_TOK + S_TOK
N_COND = 8

TM = 512
VMEM_LIMIT = 56 * 1024 * 1024


def _cparams(sem):
    return pltpu.CompilerParams(dimension_semantics=sem, vmem_limit_bytes=VMEM_LIMIT)


def _cond_index(i, tm):
    r = i * tm
    return jnp.where(r < P_TOK, 0, 1 + (r - P_TOK) // DEC_SEQ)


def _mod_spec(layer, part, tm):
    base = (layer * 6 + part) * N_COND
    return pl.BlockSpec((None, 1, D), lambda i, j: (base + _cond_index(i, tm), 0, 0))


def _row_spec(cols):
    return pl.BlockSpec((1, cols), lambda i, j: (0, 0))


def _layer_norm(r, g, b):
    mu = jnp.mean(r, axis=-1, keepdims=True)
    d = r - mu
    var = jnp.mean(d * d, axis=-1, keepdims=True)
    return d * lax.rsqrt(var + LN_EPS) * g + b


def _silu(x):
    return x * jax.nn.sigmoid(x)


def _mod_kernel(cond_ref, w_ref, b_ref, o_ref):
    s = _silu(cond_ref[...]).astype(BF16)
    o_ref[...] = jnp.dot(s, w_ref[...].astype(BF16), preferred_element_type=F32) + b_ref[...]


def _modulation(cond, w_mod, b_mod):
    tn = 1024
    return pl.pallas_call(
        _mod_kernel,
        grid=(DEPTH, 6 * D // tn),
        in_specs=[
            pl.BlockSpec((N_COND, D), lambda l, j: (0, 0)),
            pl.BlockSpec((None, D, tn), lambda l, j: (l, 0, j)),
            pl.BlockSpec((None, 1, tn), lambda l, j: (l, 0, j)),
        ],
        out_specs=pl.BlockSpec((None, N_COND, tn), lambda l, j: (l, 0, j)),
        out_shape=jax.ShapeDtypeStruct((DEPTH, N_COND, 6 * D), F32),
        compiler_params=_cparams(("parallel", "parallel")),
        name="modulation",
    )(cond, w_mod, b_mod.reshape(DEPTH, 1, 6 * D))


def _pw1_glu_kernel(x_ref, sc_ref, sh_ref, wa_ref, wg_ref, ba_ref, bg_ref, o_ref, hm_ref):
    @pl.when(pl.program_id(1) == 0)
    def _():
        hm_ref[...] = (x_ref[...] * (1.0 + sc_ref[...]) + sh_ref[...]).astype(BF16)

    hm = hm_ref[...]
    a = jnp.dot(hm, wa_ref[...], preferred_element_type=F32) + ba_ref[...]
    g = jnp.dot(hm, wg_ref[...], preferred_element_type=F32) + bg_ref[...]
    o_ref[...] = a * jax.nn.sigmoid(g)


def _pw1_glu(x, mod, w_pw1, b_pw1):
    tm, tn = TM, 512
    nj = D // tn
    return pl.pallas_call(
        _pw1_glu_kernel,
        grid=(N_TOK // tm, nj),
        in_specs=[
            pl.BlockSpec((tm, D), lambda i, j: (i, 0)),
            _mod_spec(0, 1, tm),
            _mod_spec(0, 0, tm),
            pl.BlockSpec((D, tn), lambda i, j: (0, j)),
            pl.BlockSpec((D, tn), lambda i, j: (0, j + nj)),
            pl.BlockSpec((1, tn), lambda i, j: (0, j)),
            pl.BlockSpec((1, tn), lambda i, j: (0, j + nj)),
        ],
        out_specs=pl.BlockSpec((tm, tn), lambda i, j: (i, j)),
        out_shape=jax.ShapeDtypeStruct((N_TOK, D), F32),
        scratch_shapes=[pltpu.VMEM((tm, D), BF16)],
        compiler_params=_cparams(("parallel", "arbitrary")),
        name="pw1_glu",
    )(x, mod, mod, w_pw1, w_pw1, b_pw1, b_pw1)


CONV_TB = 256
CONV_HALO = 16
CONV_CW = 256
CONV_RB = 32


def _dwconv_kernel(prev_ref, cur_ref, next_ref, w_ref, b_ref, g_ref, beta_ref, o_ref,
                   win_ref, y_ref):
    i = pl.program_id(0)
    n_p = P_TOK // CONV_TB
    per_seq = DEC_SEQ // CONV_TB
    pos = (i - n_p) % per_seq
    latent = i >= n_p
    has_prev = jnp.logical_and(latent, pos != 0)
    has_next = jnp.logical_and(latent, pos != per_seq - 1)
    win_ref[0:CONV_HALO, :] = jnp.where(has_prev, prev_ref[...], 0.0)
    win_ref[CONV_HALO:CONV_HALO + CONV_TB, :] = cur_ref[...]
    win_ref[CONV_HALO + CONV_TB:, :] = jnp.where(has_next, next_ref[...], 0.0)

    shift = CONV_HALO - CONV_WIDTH // 2

    def col_body(c, carry):
        off = pl.multiple_of(c * CONV_CW, CONV_CW)
        cols = pl.ds(off, CONV_CW)
        for r in range(0, CONV_TB, CONV_RB):
            acc = jnp.zeros((CONV_RB, CONV_CW), F32)
            for k in range(CONV_WIDTH):
                acc = acc + win_ref[pl.ds(r + shift + k, CONV_RB), cols] * w_ref[pl.ds(k, 1), cols]
            y_ref[pl.ds(r, CONV_RB), cols] = acc
        return carry

    lax.fori_loop(0, D // CONV_CW, col_body, 0)
    y = y_ref[...] + b_ref[...]
    o_ref[...] = _silu(_layer_norm(y, g_ref[...], beta_ref[...])).astype(BF16)


def _dwconv_ln_silu(u, w_dw, b_dw, cn_g, cn_b):
    hb = CONV_TB // CONV_HALO
    n_halo_blocks = N_TOK // CONV_HALO
    vec = pl.BlockSpec((1, D), lambda i: (0, 0))
    return pl.pallas_call(
        _dwconv_kernel,
        grid=(N_TOK // CONV_TB,),
        in_specs=[
            pl.BlockSpec((CONV_HALO, D), lambda i: (jnp.maximum(i * hb - 1, 0), 0)),
            pl.BlockSpec((CONV_TB, D), lambda i: (i, 0)),
            pl.BlockSpec((CONV_HALO, D), lambda i: (jnp.minimum((i + 1) * hb, n_halo_blocks - 1), 0)),
            pl.BlockSpec((CONV_WIDTH, D), lambda i: (0, 0)),
            vec, vec, vec,
        ],
        out_specs=pl.BlockSpec((CONV_TB, D), lambda i: (i, 0)),
        out_shape=jax.ShapeDtypeStruct((N_TOK, D), BF16),
        scratch_shapes=[pltpu.VMEM((CONV_TB + 2 * CONV_HALO, D), F32),
                        pltpu.VMEM((CONV_TB, D), F32)],
        compiler_params=_cparams(("parallel",)),
        name="dwconv_ln_silu",
    )(u, u, u, w_dw, b_dw, cn_g, cn_b)


def _proj_res_ln_kernel(a_ref, w_ref, b_ref, x_ref, gate_ref, lng_ref, lnb_ref, sc_ref, sh_ref,
                        xo_ref, hmo_ref, acc_ref, *, nk):
    k = pl.program_id(1)
    p = jnp.dot(a_ref[...], w_ref[...], preferred_element_type=F32)

    @pl.when(k == 0)
    def _():
        acc_ref[...] = p

    @pl.when(k > 0)
    def _():
        acc_ref[...] += p

    @pl.when(k == nk - 1)
    def _():
        y = acc_ref[...] + b_ref[...]
        xn = _layer_norm(ALPHA * x_ref[...] + gate_ref[...] * y, lng_ref[...], lnb_ref[...])
        xo_ref[...] = xn
        hmo_ref[...] = (xn * (1.0 + sc_ref[...]) + sh_ref[...]).astype(BF16)


def _proj_res_ln(a, w, b, x, mod, layer, ln_g, ln_b, tk):
    tm = TM
    kdim = a.shape[1]
    nk = kdim // tk
    return pl.pallas_call(
        functools.partial(_proj_res_ln_kernel, nk=nk),
        grid=(N_TOK // tm, nk),
        in_specs=[
            pl.BlockSpec((tm, tk), lambda i, k: (i, k)),
            pl.BlockSpec((tk, D), lambda i, k: (k, 0)),
            _row_spec(D),
            pl.BlockSpec((tm, D), lambda i, k: (i, 0)),
            _mod_spec(layer, 2, tm),
            _row_spec(D), _row_spec(D),
            _mod_spec(layer, 4, tm),
            _mod_spec(layer, 3, tm),
        ],
        out_specs=[pl.BlockSpec((tm, D), lambda i, k: (i, 0)),
                   pl.BlockSpec((tm, D), lambda i, k: (i, 0))],
        out_shape=[jax.ShapeDtypeStruct((N_TOK, D), F32),
                   jax.ShapeDtypeStruct((N_TOK, D), BF16)],
        scratch_shapes=[pltpu.VMEM((tm, D), F32)],
        compiler_params=_cparams(("parallel", "arbitrary")),
        name=f"proj_res_ln_l{layer}",
    )(a, w, b, x, mod, ln_g, ln_b, mod, mod)


def _ffn_kernel(*refs, nf, emit_hm):
    if emit_hm:
        (hm_ref, w1_ref, b1_ref, w2_ref, b2_ref, x_ref, gate_ref, lng_ref, lnb_ref,
         sc_ref, sh_ref, xo_ref, hmo_ref, acc_ref) = refs
    else:
        (hm_ref, w1_ref, b1_ref, w2_ref, b2_ref, x_ref, gate_ref, lng_ref, lnb_ref,
         xo_ref, acc_ref) = refs
    f = pl.program_id(1)
    h = jnp.dot(hm_ref[...], w1_ref[...], preferred_element_type=F32) + b1_ref[...]
    h = jnp.square(jnp.maximum(h, 0.0)).astype(BF16)
    p = jnp.dot(h, w2_ref[...], preferred_element_type=F32)

    @pl.when(f == 0)
    def _():
        acc_ref[...] = p

    @pl.when(f > 0)
    def _():
        acc_ref[...] += p

    @pl.when(f == nf - 1)
    def _():
        y = acc_ref[...] + b2_ref[...]
        xn = _layer_norm(ALPHA * x_ref[...] + gate_ref[...] * y, lng_ref[...], lnb_ref[...])
        xo_ref[...] = xn
        if emit_hm:
            hmo_ref[...] = (xn * (1.0 + sc_ref[...]) + sh_ref[...]).astype(BF16)


def _ffn(hm, w1, b1, w2, b2, x, mod, layer, ln_g, ln_b, emit_hm):
    tm, tf = TM, 512
    nf = D_FF // tf
    in_specs = [
        pl.BlockSpec((tm, D), lambda i, f: (i, 0)),
        pl.BlockSpec((D, tf), lambda i, f: (0, f)),
        pl.BlockSpec((1, tf), lambda i, f: (0, f)),
        pl.BlockSpec((tf, D), lambda i, f: (f, 0)),
        _row_spec(D),
        pl.BlockSpec((tm, D), lambda i, f: (i, 0)),
        _mod_spec(layer, 5, tm),
        _row_spec(D), _row_spec(D),
    ]
    args = [hm, w1, b1, w2, b2, x, mod, ln_g, ln_b]
    out_specs = [pl.BlockSpec((tm, D), lambda i, f: (i, 0))]
    out_shape = [jax.ShapeDtypeStruct((N_TOK, D), F32)]
    if emit_hm:
        in_specs += [_mod_spec(layer + 1, 1, tm), _mod_spec(layer + 1, 0, tm)]
        args += [mod, mod]
        out_specs.append(pl.BlockSpec((tm, D), lambda i, f: (i, 0)))
        out_shape.append(jax.ShapeDtypeStruct((N_TOK, D), BF16))
    return pl.pallas_call(
        functools.partial(_ffn_kernel, nf=nf, emit_hm=emit_hm),
        grid=(N_TOK // tm, nf),
        in_specs=in_specs,
        out_specs=out_specs,
        out_shape=out_shape,
        scratch_shapes=[pltpu.VMEM((tm, D), F32)],
        compiler_params=_cparams(("parallel", "arbitrary")),
        name=f"ffn_l{layer}",
    )(*args)


def _rope_tile(y, cos_ref, sin_ref):
    half = DK // 2
    outs = []
    for s in range(y.shape[1] // half):
        x = y[:, s * half:(s + 1) * half]
        t = (s % 2) * half
        rot = pltpu.roll(x, half // 2, axis=1)
        outs.append(x * cos_ref[:, t:t + half] + rot * sin_ref[:, t:t + half])
    return jnp.concatenate(outs, axis=1)


def _qkv_kernel(hm_ref, w_ref, cos_ref, sin_ref, o_ref, *, tm, tn):
    i = pl.program_id(0)
    j = pl.program_id(1)
    y = jnp.dot(hm_ref[...], w_ref[...], preferred_element_type=F32)
    y = y * jnp.where(j < D // tn, DK ** -0.5, 1.0)
    use_rope = jnp.logical_and(j < 2 * D // tn, i >= P_TOK // tm)

    @pl.when(use_rope)
    def _():
        o_ref[...] = _rope_tile(y, cos_ref, sin_ref).astype(BF16)

    @pl.when(jnp.logical_not(use_rope))
    def _():
        o_ref[...] = y.astype(BF16)


def _qkv_proj(hm, w_in, cos_t, sin_t):
    tm, tn = TM, 512
    tpos = lambda i, j: ((jnp.maximum(i * tm - P_TOK, 0) % DEC_SEQ) // tm, 0)
    return pl.pallas_call(
        functools.partial(_qkv_kernel, tm=tm, tn=tn),
        grid=(N_TOK // tm, 4 * D // tn),
        in_specs=[
            pl.BlockSpec((tm, D), lambda i, j: (i, 0)),
            pl.BlockSpec((D, tn), lambda i, j: (0, j)),
            pl.BlockSpec((tm, DK), tpos),
            pl.BlockSpec((tm, DK), tpos),
        ],
        out_specs=pl.BlockSpec((tm, tn), lambda i, j: (i, j)),
        out_shape=jax.ShapeDtypeStruct((N_TOK, 4 * D), BF16),
        compiler_params=_cparams(("parallel", "parallel")),
        name="ret_qkv_proj",
    )(hm, w_in, cos_t, sin_t)


def _gate_kernel(hm_ref, w_ref, o_ref):
    o_ref[...] = jnp.dot(hm_ref[...], w_ref[...], preferred_element_type=F32)


def _gate_proj(hm, w_in):
    tm, tn = TM, 512
    joff = 4 * D // tn
    return pl.pallas_call(
        _gate_kernel,
        grid=(N_TOK // tm, 2 * D // tn),
        in_specs=[
            pl.BlockSpec((tm, D), lambda i, j: (i, 0)),
            pl.BlockSpec((D, tn), lambda i, j: (0, j + joff)),
        ],
        out_specs=pl.BlockSpec((tm, tn), lambda i, j: (i, j)),
        out_shape=jax.ShapeDtypeStruct((N_TOK, 2 * D), F32),
        compiler_params=_cparams(("parallel", "parallel")),
        name="ret_gate_proj",
    )(hm, w_in)


def _retention_kernel(*refs, seq, has_s0, write_state):
    refs = list(refs)
    rate_ref, q_ref, k_ref, v_ref, g_ref = refs[:5]
    pos = 5
    s0_ref = None
    if has_s0:
        s0_ref = refs[pos]
        pos += 1
    o_ref = refs[pos]
    pos += 1
    so_ref = None
    if write_state:
        so_ref = refs[pos]
        pos += 1
    acc_ref, s_ref = refs[pos:]

    c = CHUNK
    n = seq // c
    lg = jnp.log1p(-jnp.exp2(rate_ref[...]))
    lgf, lgb = lg[0], lg[1]
    rows_v = lax.broadcasted_iota(jnp.int32, (c, DV), 0).astype(F32)
    rows_k = lax.broadcasted_iota(jnp.int32, (c, DK), 0).astype(F32)
    diff = (lax.broadcasted_iota(jnp.int32, (c, c), 0)
            - lax.broadcasted_iota(jnp.int32, (c, c), 1)).astype(F32)

    din_f = jnp.exp(lgf * (rows_v + 1.0))
    dout_f = jnp.exp(lgf[:, :DK] * (c - 1.0 - rows_k))
    dchunk_f = jnp.exp(lgf * float(c))
    mask_f = jnp.where(diff >= 0, jnp.exp(lgf[:, :c] * jnp.maximum(diff, 0.0)), 0.0)
    din_b = jnp.exp(lgb * (float(c) - rows_v))
    dout_b = jnp.exp(lgb[:, :DK] * rows_k)
    dchunk_b = jnp.exp(lgb * float(c))
    mask_b = jnp.where(diff <= 0, jnp.exp(lgb[:, :c] * jnp.maximum(-diff, 0.0)), 0.0)

    def chunk_step(ci, mask, din, dout, dchunk):
        rows = pl.ds(pl.multiple_of(ci * c, c), c)
        qc = q_ref[rows, :]
        kc = k_ref[rows, :]
        vc = v_ref[rows, :]
        scores = lax.dot_general(qc, kc, (((1,), (1,)), ((), ())), preferred_element_type=F32)
        inner = jnp.dot((scores * mask).astype(BF16), vc, preferred_element_type=F32)
        s = s_ref[...]
        cross = jnp.dot(qc, s.astype(BF16), preferred_element_type=F32) * din
        kd = (kc.astype(F32) * dout).astype(BF16)
        s_ref[...] = dchunk * s + lax.dot_general(
            kd, vc, (((0,), (0,)), ((), ())), preferred_element_type=F32)
        return rows, inner + cross

    def init_state(direction):
        if has_s0:
            s_ref[...] = s0_ref[direction]
        else:
            s_ref[...] = jnp.zeros((DK, DV), F32)

    init_state(0)

    def fwd(ci, carry):
        rows, o = chunk_step(ci, mask_f, din_f, dout_f, dchunk_f)
        acc_ref[rows, :] = o
        return carry

    lax.fori_loop(0, n, fwd, 0)
    if write_state:
        so_ref[0] = s_ref[...]

    init_state(1)

    def bwd(t, carry):
        rows, o = chunk_step(n - 1 - t, mask_b, din_b, dout_b, dchunk_b)
        tot = acc_ref[rows, :] + o
        mu = jnp.mean(tot, axis=-1, keepdims=True)
        d = tot - mu
        var = jnp.mean(d * d, axis=-1, keepdims=True)
        nrm = d * lax.rsqrt(var + GN_EPS)
        o_ref[rows, :] = (_silu(g_ref[rows, :]) * nrm).astype(BF16)
        return carry

    lax.fori_loop(0, n, bwd, 0)
    if write_state:
        so_ref[1] = s_ref[...]


def _retention(rate, qkv, g, s0, out_init, *, seq, n_seq, row_block0, write_state):
    has_s0 = s0 is not None
    qoff, koff, voff = 0, D // DK, 2 * D // DV
    state_block = (None, None, 2, None, DK, DV)
    in_specs = [
        pl.BlockSpec((2, None, 1, DV), lambda b, h: (0, h, 0, 0)),
        pl.BlockSpec((seq, DK), lambda b, h: (row_block0 + b, qoff + h)),
        pl.BlockSpec((seq, DK), lambda b, h: (row_block0 + b, koff + h)),
        pl.BlockSpec((seq, DV), lambda b, h: (row_block0 + b, voff + h)),
        pl.BlockSpec((seq, DV), lambda b, h: (row_block0 + b, h)),
    ]
    args = [rate, qkv, qkv, qkv, g]
    if has_s0:
        in_specs.append(pl.BlockSpec(state_block, lambda b, h: (b, 0, 0, h, 0, 0)))
        args.append(s0)
    in_specs.append(pl.BlockSpec(memory_space=pl.ANY))
    args.append(out_init)
    alias_idx = len(args) - 1
    out_specs = [pl.BlockSpec((seq, DV), lambda b, h: (row_block0 + b, h))]
    out_shape = [jax.ShapeDtypeStruct((N_TOK, 2 * D), BF16)]
    if write_state:
        out_specs.append(pl.BlockSpec(state_block, lambda b, h: (b, 0, 0, h, 0, 0)))
        out_shape.append(jax.ShapeDtypeStruct((n_seq, 1, 2, HEADS, DK, DV), F32))

    def body(*refs):
        refs = list(refs)
        del refs[alias_idx]
        _retention_kernel(*refs, seq=seq, has_s0=has_s0, write_state=write_state)

    return pl.pallas_call(
        body,
        grid=(n_seq, HEADS),
        in_specs=in_specs,
        out_specs=out_specs,
        out_shape=out_shape,
        input_output_aliases={alias_idx: 0},
        scratch_shapes=[pltpu.VMEM((seq, DV), F32), pltpu.VMEM((DK, DV), F32)],
        compiler_params=_cparams(("parallel", "parallel")),
        name=f"retention_t{seq}",
    )(*args)


def _rope_tables():
    t = jnp.arange(DEC_SEQ)
    row = (t // GRID_W).astype(F32)
    col = (t % GRID_W).astype(F32)
    quarter = DK // 4
    inv = ROPE_BASE ** (-jnp.arange(quarter, dtype=F32) / quarter)
    ang_r = row[:, None] * inv
    ang_c = col[:, None] * inv
    cos_t = jnp.concatenate([jnp.cos(ang_r)] * 2 + [jnp.cos(ang_c)] * 2, axis=1)
    sin_t = jnp.concatenate([-jnp.sin(ang_r), jnp.sin(ang_r), -jnp.sin(ang_c), jnp.sin(ang_c)], axis=1)
    return cos_t, sin_t


def kernel(x_prompt, x_sample, state_ret, c, c_ctx, w_mod, b_mod, ln1_g, ln1_b, ln2_g, ln2_b,
           w_pw1, b_pw1, w_dw, b_dw, cn_g, cn_b, w_pw2, b_pw2,
           w_ret_in, ret_log2_rate, w_ret_o, w_ff1, b_ff1, w_ff2, b_ff2):
    x = jnp.concatenate([x_prompt.reshape(P_TOK, D), x_sample.reshape(S_TOK, D)], axis=0)
    cond = jnp.concatenate(
        [c_ctx[None, :], c, jnp.zeros((N_COND - 1 - DEC_BATCH, D), F32)], axis=0)

    mod = _modulation(cond, w_mod, b_mod)
    mod = mod.reshape(DEPTH, N_COND, 6, D).transpose(0, 2, 1, 3).reshape(DEPTH * 6 * N_COND, 1, D)

    bf = lambda w: w.astype(BF16)
    row = lambda v: v.reshape(1, -1)

    u = _pw1_glu(x, mod, bf(w_pw1[0]), row(b_pw1[0]))
    v = _dwconv_ln_silu(u, w_dw[0], row(b_dw[0]), row(cn_g[0]), row(cn_b[0]))
    x, hm = _proj_res_ln(v, bf(w_pw2[0]), row(b_pw2[0]), x, mod, 0,
                         row(ln1_g[0]), row(ln1_b[0]), tk=1024)
    x, hm = _ffn(hm, bf(w_ff1[0]), row(b_ff1[0]), bf(w_ff2[0]), row(b_ff2[0]), x, mod, 0,
                 row(ln2_g[0]), row(ln2_b[0]), emit_hm=True)

    w_in = bf(w_ret_in[0])
    cos_t, sin_t = _rope_tables()
    qkv = _qkv_proj(hm, w_in, cos_t, sin_t)
    g = _gate_proj(hm, w_in)
    rate = jnp.broadcast_to(ret_log2_rate[0][:, :, None, None], (2, HEADS, 1, DV))
    gated = jnp.zeros((N_TOK, 2 * D), BF16)
    gated, new_state = _retention(rate, qkv, g, None, gated, seq=SEQ, n_seq=BATCH,
                                  row_block0=0, write_state=True)
    (gated,) = _retention(rate, qkv, g, state_ret, gated, seq=DEC_SEQ, n_seq=DEC_BATCH,
                          row_block0=P_TOK // DEC_SEQ, write_state=False)
    x, hm = _proj_res_ln(gated, bf(w_ret_o[0]), jnp.zeros((1, D), F32), x, mod, 1,
                         row(ln1_g[1]), row(ln1_b[1]), tk=1024)
    (x,) = _ffn(hm, bf(w_ff1[1]), row(b_ff1[1]), bf(w_ff2[1]), row(b_ff2[1]), x, mod, 1,
                row(ln2_g[1]), row(ln2_b[1]), emit_hm=False)

    y_prompt = x[:P_TOK].reshape(BATCH, SEQ, D)
    y_sample = x[P_TOK:].reshape(DEC_BATCH, DEC_SEQ, D)
    return y_prompt, y_sample, new_state
```

```python
import functools

import jax
import jax.numpy as jnp
from jax import lax
from jax.experimental import pallas as pl
from jax.experimental.pallas import tpu as pltpu

F32 = jnp.float32
BF16 = jnp.bfloat16

D = 2048
BATCH = 16
SEQ = 256
DEC_BATCH = 4
DEC_SEQ = 2048
GRID_W = 64
CONV_WIDTH = 31
D_FF = 4 * D
HEADS = 8
DK = D // HEADS
DV = 2 * D // HEADS
ROPE_BASE = 10000.0
LN_EPS = 1e-5
GN_EPS = 1e-5
DEPTH = 2
ALPHA = (2.0 * DEPTH) ** 0.25

P_TOK = BATCH * SEQ
S_TOK = DEC_BATCH * DEC_SEQ
N_TOK = P_TOK + S_TOK
N_COND = 8

LANES = 128
SUBLANES = 8
TM = 512
VMEM_LIMIT = 56 * 1024 * 1024


def _cparams(sem):
    return pltpu.CompilerParams(dimension_semantics=sem, vmem_limit_bytes=VMEM_LIMIT)


def _cond_index(i, tm):
    r = i * tm
    return jnp.where(r < P_TOK, 0, 1 + (r - P_TOK) // DEC_SEQ)


def _mod_spec(layer, part, tm, row_axis=0):
    base = (layer * 6 + part) * N_COND
    return pl.BlockSpec((None, 1, D),
                        lambda *ids: (base + _cond_index(ids[row_axis], tm), 0, 0))


def _const_spec(shape):
    return pl.BlockSpec(shape, lambda *ids: (0,) * len(shape))


def _group_specs(tm, cols, row_axis=0):
    n_p = P_TOK // tm
    ctx = pl.BlockSpec((tm, cols), lambda *ids: (jnp.minimum(ids[row_axis], n_p - 1), 0))
    lat = pl.BlockSpec((tm, cols), lambda *ids: (jnp.maximum(ids[row_axis] - n_p, 0), 0))
    return [ctx, lat]


def _layer_norm(r, g, b):
    mu = jnp.mean(r, axis=-1, keepdims=True)
    d = r - mu
    var = jnp.mean(d * d, axis=-1, keepdims=True)
    return d * lax.rsqrt(var + LN_EPS) * g + b


def _silu(x):
    return x * jax.nn.sigmoid(x)


def _dot(a, b):
    return jnp.dot(a, b, preferred_element_type=F32)


def _mod_kernel(cond_ref, w_ref, b_ref, o_ref):
    s = _silu(cond_ref[...]).astype(BF16)
    o_ref[...] = _dot(s, w_ref[...].astype(BF16)) + b_ref[...]


def _modulation(cond, w_mod, b_mod):
    tn = 1024
    return pl.pallas_call(
        _mod_kernel,
        grid=(DEPTH, 6 * D // tn),
        in_specs=[
            pl.BlockSpec((N_COND, D), lambda l, j: (0, 0)),
            pl.BlockSpec((None, D, tn), lambda l, j: (l, 0, j)),
            pl.BlockSpec((None, 1, tn), lambda l, j: (l, 0, j)),
        ],
        out_specs=pl.BlockSpec((None, N_COND, tn), lambda l, j: (l, 0, j)),
        out_shape=jax.ShapeDtypeStruct((DEPTH, N_COND, 6 * D), F32),
        compiler_params=_cparams(("parallel", "parallel")),
        name="modulation",
    )(cond, w_mod, b_mod.reshape(DEPTH, 1, 6 * D))


def _pw1_glu_kernel(xp_ref, xs_ref, sc_ref, sh_ref, wa_ref, wg_ref, ba_ref, bg_ref, o_ref,
                    wa_bf, wg_bf, *, n_p):
    i = pl.program_id(1)

    @pl.when(i == 0)
    def _():
        wa_bf[...] = wa_ref[...].astype(BF16)
        wg_bf[...] = wg_ref[...].astype(BF16)

    x = jnp.where(i < n_p, xp_ref[...], xs_ref[...])
    hm = (x * (1.0 + sc_ref[...]) + sh_ref[...]).astype(BF16)
    a = _dot(hm, wa_bf[...]) + ba_ref[...]
    g = _dot(hm, wg_bf[...]) + bg_ref[...]
    o_ref[...] = a * jax.nn.sigmoid(g)


def _pw1_glu(xp, xs, mod, w_pw1, b_pw1):
    tm, tn = TM, 512
    nj = D // tn
    return pl.pallas_call(
        functools.partial(_pw1_glu_kernel, n_p=P_TOK // tm),
        grid=(nj, N_TOK // tm),
        in_specs=_group_specs(tm, D, row_axis=1) + [
            _mod_spec(0, 1, tm, row_axis=1),
            _mod_spec(0, 0, tm, row_axis=1),
            pl.BlockSpec((D, tn), lambda j, i: (0, j)),
            pl.BlockSpec((D, tn), lambda j, i: (0, j + nj)),
            pl.BlockSpec((1, tn), lambda j, i: (0, j)),
            pl.BlockSpec((1, tn), lambda j, i: (0, j + nj)),
        ],
        out_specs=pl.BlockSpec((tm, tn), lambda j, i: (i, j)),
        out_shape=jax.ShapeDtypeStruct((N_TOK, D), F32),
        scratch_shapes=[pltpu.VMEM((D, tn), BF16), pltpu.VMEM((D, tn), BF16)],
        compiler_params=_cparams(("parallel", "arbitrary")),
        name="pw1_glu",
    )(xp, xs, mod, mod, w_pw1, w_pw1, b_pw1, b_pw1)


CONV_TB = 256
CONV_HALO = 16
CONV_WIN = CONV_TB + 2 * CONV_HALO
CONV_CW = 256
CONV_RB = 32


def _dwconv_kernel(prev_ref, cur_ref, next_ref, w_ref, b_ref, g_ref, beta_ref, o_ref,
                   win_ref, sh_ref, y_ref):
    i = pl.program_id(0)
    n_p = P_TOK // CONV_TB
    per_seq = DEC_SEQ // CONV_TB
    pos = (i - n_p) % per_seq
    latent = i >= n_p
    has_prev = jnp.logical_and(latent, pos != 0)
    has_next = jnp.logical_and(latent, pos != per_seq - 1)
    win_ref[0:CONV_HALO, :] = jnp.where(has_prev, prev_ref[...], 0.0)
    win_ref[CONV_HALO:CONV_HALO + CONV_TB, :] = cur_ref[...]
    win_ref[CONV_HALO + CONV_TB:, :] = jnp.where(has_next, next_ref[...], 0.0)

    shift = CONV_HALO - CONV_WIDTH // 2

    def col_body(c, carry):
        cols = pl.ds(pl.multiple_of(c * CONV_CW, CONV_CW), CONV_CW)
        w0 = win_ref[:, cols]
        for s in range(1, SUBLANES):
            sh_ref[s - 1] = pltpu.roll(w0, CONV_WIN - s, axis=0)
        for r in range(0, CONV_TB, CONV_RB):
            acc = jnp.zeros((CONV_RB, CONV_CW), F32)
            for k in range(CONV_WIDTH):
                off = shift + k
                s, base = off % SUBLANES, r + off - off % SUBLANES
                if s == 0:
                    src = win_ref[pl.ds(base, CONV_RB), cols]
                else:
                    src = sh_ref[s - 1, pl.ds(base, CONV_RB), :]
                acc = acc + src * w_ref[pl.ds(k, 1), cols]
            y_ref[pl.ds(r, CONV_RB), cols] = acc
        return carry

    lax.fori_loop(0, D // CONV_CW, col_body, 0)
    y = y_ref[...] + b_ref[...]
    o_ref[...] = _silu(_layer_norm(y, g_ref[...], beta_ref[...])).astype(BF16)


def _dwconv_ln_silu(u, w_dw, b_dw, cn_g, cn_b):
    hb = CONV_TB // CONV_HALO
    n_halo_blocks = N_TOK // CONV_HALO
    vec = pl.BlockSpec((1, D), lambda i: (0, 0))
    return pl.pallas_call(
        _dwconv_kernel,
        grid=(N_TOK // CONV_TB,),
        in_specs=[
            pl.BlockSpec((CONV_HALO, D), lambda i: (jnp.maximum(i * hb - 1, 0), 0)),
            pl.BlockSpec((CONV_TB, D), lambda i: (i, 0)),
            pl.BlockSpec((CONV_HALO, D), lambda i: (jnp.minimum((i + 1) * hb, n_halo_blocks - 1), 0)),
            pl.BlockSpec((CONV_WIDTH, D), lambda i: (0, 0)),
            vec, vec, vec,
        ],
        out_specs=pl.BlockSpec((CONV_TB, D), lambda i: (i, 0)),
        out_shape=jax.ShapeDtypeStruct((N_TOK, D), BF16),
        scratch_shapes=[pltpu.VMEM((CONV_WIN, D), F32),
                        pltpu.VMEM((SUBLANES - 1, CONV_WIN, CONV_CW), F32),
                        pltpu.VMEM((CONV_TB, D), F32)],
        compiler_params=_cparams(("parallel",)),
        name="dwconv_ln_silu",
    )(u, u, u, w_dw, b_dw, cn_g, cn_b)


PROJ_SUB = 256


def _proj_res_ln_kernel(*refs, nk, n_p, lhs_split, res_split, tm):
    refs = list(refs)
    a_refs = [refs.pop(0) for _ in range(2 if lhs_split else 1)]
    w_ref, b_ref = refs.pop(0), refs.pop(0)
    x_refs = [refs.pop(0) for _ in range(2 if res_split else 1)]
    gate_ref, lng_ref, lnb_ref, sc_ref, sh_ref, xo_ref, hmo_ref = refs[:7]
    acc_ref = refs[7] if nk > 1 else None
    i = pl.program_id(0)
    k = pl.program_id(1)

    def pick(pair, rows):
        if len(pair) == 2:
            return jnp.where(i < n_p, pair[0][rows, :], pair[1][rows, :])
        return pair[0][rows, :]

    if nk > 1:
        @pl.when(k == 0)
        def _():
            acc_ref[...] = _dot(pick(a_refs, pl.ds(0, tm)), w_ref[...])

        @pl.when(jnp.logical_and(k > 0, k < nk - 1))
        def _():
            acc_ref[...] += _dot(pick(a_refs, pl.ds(0, tm)), w_ref[...])

    @pl.when(k == nk - 1)
    def _():
        for s in range(tm // PROJ_SUB):
            rows = pl.ds(s * PROJ_SUB, PROJ_SUB)
            y = _dot(pick(a_refs, rows), w_ref[...]) + b_ref[...]
            if nk > 1:
                y = y + acc_ref[rows, :]
            xn = _layer_norm(ALPHA * pick(x_refs, rows) + gate_ref[...] * y,
                             lng_ref[...], lnb_ref[...])
            xo_ref[rows, :] = xn
            hmo_ref[rows, :] = (xn * (1.0 + sc_ref[...]) + sh_ref[...]).astype(BF16)


def _proj_res_ln(a_list, w, b, x_list, mod, layer, ln_g, ln_b, tk):
    tm = TM
    kdim = w.shape[0]
    nk = kdim // tk
    n_p = P_TOK // tm
    lhs_split, res_split = len(a_list) == 2, len(x_list) == 2

    def row_specs(split, cols, kmap):
        if not split:
            return [pl.BlockSpec((tm, cols), lambda i, k: (i, kmap(k)))]
        return [pl.BlockSpec((tm, cols), lambda i, k: (jnp.minimum(i, n_p - 1), kmap(k))),
                pl.BlockSpec((tm, cols), lambda i, k: (jnp.maximum(i - n_p, 0), kmap(k)))]

    in_specs = (row_specs(lhs_split, tk, lambda k: k)
                + [pl.BlockSpec((tk, D), lambda i, k: (k, 0)), _const_spec((1, D))]
                + row_specs(res_split, D, lambda k: 0)
                + [_mod_spec(layer, 2, tm), _const_spec((1, D)), _const_spec((1, D)),
                   _mod_spec(layer, 4, tm), _mod_spec(layer, 3, tm)])
    return pl.pallas_call(
        functools.partial(_proj_res_ln_kernel, nk=nk, n_p=n_p, lhs_split=lhs_split,
                          res_split=res_split, tm=tm),
        grid=(N_TOK // tm, nk),
        in_specs=in_specs,
        out_specs=[pl.BlockSpec((tm, D), lambda i, k: (i, 0)),
                   pl.BlockSpec((tm, D), lambda i, k: (i, 0))],
        out_shape=[jax.ShapeDtypeStruct((N_TOK, D), F32),
                   jax.ShapeDtypeStruct((N_TOK, D), BF16)],
        scratch_shapes=[pltpu.VMEM((tm, D), F32)] if nk > 1 else [],
        compiler_params=_cparams(("parallel", "arbitrary")),
        name=f"proj_res_ln_l{layer}",
    )(*a_list, w, b, *x_list, mod, ln_g, ln_b, mod, mod)


def _ffn_kernel(*refs, nf, n_p, last):
    hm_ref, w1_ref, b1_ref, w2_ref, b2_ref, x_ref, gate_ref, lng_ref, lnb_ref = refs[:9]
    if last:
        yp_ref, ys_ref, acc_ref = refs[9:]
    else:
        sc_ref, sh_ref, xo_ref, hmo_ref, acc_ref = refs[9:]
    i = pl.program_id(0)
    f = pl.program_id(1)

    @pl.when(f == 0)
    def _():
        acc_ref[...] = jnp.zeros(acc_ref.shape, F32)

    h = _dot(hm_ref[...], w1_ref[...]) + b1_ref[...]
    h = jnp.square(jnp.maximum(h, 0.0)).astype(BF16)
    acc_ref[...] += _dot(h, w2_ref[...])

    @pl.when(f == nf - 1)
    def _():
        y = acc_ref[...] + b2_ref[...]
        xn = _layer_norm(ALPHA * x_ref[...] + gate_ref[...] * y, lng_ref[...], lnb_ref[...])
        if last:
            @pl.when(i < n_p)
            def _():
                yp_ref[...] = xn

            @pl.when(i >= n_p)
            def _():
                ys_ref[...] = xn
        else:
            xo_ref[...] = xn
            hmo_ref[...] = (xn * (1.0 + sc_ref[...]) + sh_ref[...]).astype(BF16)


def _ffn(hm, w1_tiles, b1, w2, b2, x, mod, layer, ln_g, ln_b, last):
    tm = TM
    nf, _, tf = w1_tiles.shape
    n_p = P_TOK // tm
    in_specs = [
        pl.BlockSpec((tm, D), lambda i, f: (i, 0)),
        pl.BlockSpec((None, D, tf), lambda i, f: (f, 0, 0)),
        pl.BlockSpec((1, tf), lambda i, f: (0, f)),
        pl.BlockSpec((tf, D), lambda i, f: (f, 0)),
        _const_spec((1, D)),
        pl.BlockSpec((tm, D), lambda i, f: (i, 0)),
        _mod_spec(layer, 5, tm),
        _const_spec((1, D)), _const_spec((1, D)),
    ]
    args = [hm, w1_tiles, b1, w2, b2, x, mod, ln_g, ln_b]
    if last:
        out_specs = _group_specs(tm, D)
        out_shape = [jax.ShapeDtypeStruct((P_TOK, D), F32), jax.ShapeDtypeStruct((S_TOK, D), F32)]
    else:
        in_specs += [_mod_spec(layer + 1, 1, tm), _mod_spec(layer + 1, 0, tm)]
        args += [mod, mod]
        out_specs = [pl.BlockSpec((tm, D), lambda i, f: (i, 0))] * 2
        out_shape = [jax.ShapeDtypeStruct((N_TOK, D), F32), jax.ShapeDtypeStruct((N_TOK, D), BF16)]
    return pl.pallas_call(
        functools.partial(_ffn_kernel, nf=nf, n_p=n_p, last=last),
        grid=(N_TOK // tm, nf),
        in_specs=in_specs,
        out_specs=out_specs,
        out_shape=out_shape,
        scratch_shapes=[pltpu.VMEM((tm, D), F32)],
        compiler_params=_cparams(("arbitrary" if last else "parallel", "arbitrary")),
        name=f"ffn_l{layer}",
    )(*args)


def _ffn_w1_tiles(w1, tf):
    return w1.astype(BF16).reshape(D, D_FF // tf, tf).transpose(1, 0, 2)


def _rope_tile(y, cos_ref, sin_ref):
    half = DK // 2
    outs = []
    for s in range(y.shape[1] // half):
        x = y[:, s * half:(s + 1) * half]
        t = (s % 2) * half
        rot = pltpu.roll(x, half // 2, axis=1)
        outs.append(x * cos_ref[:, t:t + half] + rot * sin_ref[:, t:t + half])
    return jnp.concatenate(outs, axis=1)


def _ret_in_kernel(hm_ref, w_ref, cos_ref, sin_ref, o_ref, w_bf, *, tm, tn):
    j = pl.program_id(0)
    i = pl.program_id(1)

    @pl.when(i == 0)
    def _():
        w_bf[...] = w_ref[...].astype(BF16)

    y = _dot(hm_ref[...], w_bf[...])
    y = y * jnp.where(j < D // tn, DK ** -0.5, 1.0)
    use_rope = jnp.logical_and(j < 2 * D // tn, i >= P_TOK // tm)

    @pl.when(use_rope)
    def _():
        o_ref[...] = _rope_tile(y, cos_ref, sin_ref).astype(BF16)

    @pl.when(jnp.logical_not(use_rope))
    def _():
        o_ref[...] = y.astype(BF16)


def _ret_in_proj(hm, w_in, cos_t, sin_t):
    tm, tn = 1024, 1024
    n_p = P_TOK // tm
    tpos = lambda j, i: (jnp.maximum(i - n_p, 0) % (DEC_SEQ // tm), 0)
    return pl.pallas_call(
        functools.partial(_ret_in_kernel, tm=tm, tn=tn),
        grid=(6 * D // tn, N_TOK // tm),
        in_specs=[
            pl.BlockSpec((tm, D), lambda j, i: (i, 0)),
            pl.BlockSpec((D, tn), lambda j, i: (0, j)),
            pl.BlockSpec((tm, DK), tpos),
            pl.BlockSpec((tm, DK), tpos),
        ],
        out_specs=pl.BlockSpec((tm, tn), lambda j, i: (i, j)),
        out_shape=jax.ShapeDtypeStruct((N_TOK, 6 * D), BF16),
        scratch_shapes=[pltpu.VMEM((D, tn), BF16)],
        compiler_params=_cparams(("parallel", "arbitrary")),
        name="ret_in_proj",
    )(hm, w_in, cos_t, sin_t)


RC = 256


def _scale_rows(a, d):
    return jnp.concatenate(
        [a[:, s * LANES:(s + 1) * LANES] * d for s in range(a.shape[1] // LANES)], axis=1)


def _retention_kernel(*refs, seq, has_s0, write_state):
    refs = list(refs)
    rate_ref, q_ref, k_ref, v_ref, g_ref = [refs.pop(0) for _ in range(5)]
    s0_ref = refs.pop(0) if has_s0 else None
    o_ref = refs.pop(0)
    so_ref = refs.pop(0) if write_state else None
    mask_ref, din_ref, dout_ref, sb_ref, s_ref = refs
    n = seq // RC
    c_f = float(RC)

    lg = jnp.log1p(-jnp.exp2(rate_ref[...]))
    lgf, lgb = lg[0], lg[1]
    dchunk_f = jnp.exp(lgf * c_f)
    dchunk_b = jnp.exp(lgb * c_f)

    @pl.when(pl.program_id(1) == 0)
    def _():
        rows = lax.broadcasted_iota(jnp.int32, (RC, LANES), 0).astype(F32)
        diff = (lax.broadcasted_iota(jnp.int32, (RC, RC), 0)
                - lax.broadcasted_iota(jnp.int32, (RC, RC), 1)).astype(F32)
        mf = jnp.where(diff >= 0, jnp.exp(lgf[:, :RC] * jnp.maximum(diff, 0.0)), 0.0)
        mb = jnp.where(diff <= 0, jnp.exp(lgb[:, :RC] * jnp.maximum(-diff, 0.0)), 0.0)
        mask_ref[...] = mf + mb
        lf, lb = lgf[:, :LANES], lgb[:, :LANES]
        din_ref[0] = jnp.exp(lf * (rows + 1.0))
        din_ref[1] = jnp.exp(lb * (c_f - rows))
        dout_ref[0] = jnp.exp(lf * (c_f - 1.0 - rows))
        dout_ref[1] = jnp.exp(lb * rows)

    def chunk_rows(c):
        if isinstance(c, int):
            return pl.ds(c * RC, RC)
        return pl.ds(pl.multiple_of(c * RC, RC), RC)

    def init_state(direction):
        if has_s0:
            s_ref[...] = s0_ref[direction]
        else:
            s_ref[...] = jnp.zeros((DK, DV), F32)

    def state_update(c, direction, dchunk):
        rows = chunk_rows(c)
        kd = _scale_rows(k_ref[rows, :].astype(F32), dout_ref[direction]).astype(BF16)
        upd = lax.dot_general(kd, v_ref[rows, :], (((0,), (0,)), ((), ())),
                              preferred_element_type=F32)
        s_ref[...] = dchunk * s_ref[...] + upd

    def loop(body):
        if n <= 2:
            for t in range(n):
                body(t)
        else:
            lax.fori_loop(0, n, lambda t, carry: (body(t), carry)[1], 0)

    init_state(1)

    def pass1(t):
        c = n - 1 - t
        sb_ref[c] = s_ref[...].astype(BF16)
        state_update(c, 1, dchunk_b)

    loop(pass1)
    if write_state:
        so_ref[1] = s_ref[...]

    init_state(0)

    def pass2(c):
        rows = chunk_rows(c)
        qc = q_ref[rows, :]
        scores = lax.dot_general(qc, k_ref[rows, :], (((1,), (1,)), ((), ())),
                                 preferred_element_type=F32)
        tot = _dot((scores * mask_ref[...]).astype(BF16), v_ref[rows, :])
        if has_s0 or not (isinstance(c, int) and c == 0):
            tot = tot + _scale_rows(_dot(qc, s_ref[...].astype(BF16)), din_ref[0])
        if has_s0 or not (isinstance(c, int) and c == n - 1):
            tot = tot + _scale_rows(_dot(qc, sb_ref[c]), din_ref[1])
        mu = jnp.mean(tot, axis=-1, keepdims=True)
        d = tot - mu
        var = jnp.mean(d * d, axis=-1, keepdims=True)
        nrm = d * lax.rsqrt(var + GN_EPS)
        o_ref[rows, :] = (_silu(g_ref[rows, :].astype(F32)) * nrm).astype(BF16)
        state_update(c, 0, dchunk_f)

    loop(pass2)
    if write_state:
        so_ref[0] = s_ref[...]


def _retention(rate, proj, s0, *, seq, n_seq, row_block0, write_state):
    has_s0 = s0 is not None
    koff, voff, goff = D // DK, 2 * D // DV, 4 * D // DV
    state_block = (None, None, 2, None, DK, DV)
    in_specs = [
        pl.BlockSpec((2, None, 1, DV), lambda h, b: (0, h, 0, 0)),
        pl.BlockSpec((seq, DK), lambda h, b: (row_block0 + b, h)),
        pl.BlockSpec((seq, DK), lambda h, b: (row_block0 + b, koff + h)),
        pl.BlockSpec((seq, DV), lambda h, b: (row_block0 + b, voff + h)),
        pl.BlockSpec((seq, DV), lambda h, b: (row_block0 + b, goff + h)),
    ]
    args = [rate, proj, proj, proj, proj]
    if has_s0:
        in_specs.append(pl.BlockSpec(state_block, lambda h, b: (b, 0, 0, h, 0, 0)))
        args.append(s0)
    out_specs = [pl.BlockSpec((seq, DV), lambda h, b: (b, h))]
    out_shape = [jax.ShapeDtypeStruct((n_seq * seq, 2 * D), BF16)]
    if write_state:
        out_specs.append(pl.BlockSpec(state_block, lambda h, b: (b, 0, 0, h, 0, 0)))
        out_shape.append(jax.ShapeDtypeStruct((n_seq, 1, 2, HEADS, DK, DV), F32))
    return pl.pallas_call(
        functools.partial(_retention_kernel, seq=seq, has_s0=has_s0, write_state=write_state),
        grid=(HEADS, n_seq),
        in_specs=in_specs,
        out_specs=out_specs,
        out_shape=out_shape,
        scratch_shapes=[pltpu.VMEM((RC, RC), F32),
                        pltpu.VMEM((2, RC, LANES), F32),
                        pltpu.VMEM((2, RC, LANES), F32),
                        pltpu.VMEM((seq // RC, DK, DV), BF16),
                        pltpu.VMEM((DK, DV), F32)],
        compiler_params=_cparams(("parallel", "arbitrary")),
        name=f"retention_t{seq}",
    )(*args)


def _rope_tables():
    t = jnp.arange(DEC_SEQ)
    row = (t // GRID_W).astype(F32)
    col = (t % GRID_W).astype(F32)
    quarter = DK // 4
    inv = ROPE_BASE ** (-jnp.arange(quarter, dtype=F32) / quarter)
    ang_r = row[:, None] * inv
    ang_c = col[:, None] * inv
    cos_t = jnp.concatenate([jnp.cos(ang_r)] * 2 + [jnp.cos(ang_c)] * 2, axis=1)
    sin_t = jnp.concatenate([-jnp.sin(ang_r), jnp.sin(ang_r), -jnp.sin(ang_c), jnp.sin(ang_c)], axis=1)
    return cos_t, sin_t


def kernel(x_prompt, x_sample, state_ret, c, c_ctx, w_mod, b_mod, ln1_g, ln1_b, ln2_g, ln2_b,
           w_pw1, b_pw1, w_dw, b_dw, cn_g, cn_b, w_pw2, b_pw2,
           w_ret_in, ret_log2_rate, w_ret_o, w_ff1, b_ff1, w_ff2, b_ff2):
    xp = x_prompt.reshape(P_TOK, D)
    xs = x_sample.reshape(S_TOK, D)
    cond = jnp.concatenate(
        [c_ctx[None, :], c, jnp.zeros((N_COND - 1 - DEC_BATCH, D), F32)], axis=0)

    mod = _modulation(cond, w_mod, b_mod)
    mod = mod.reshape(DEPTH, N_COND, 6, D).transpose(0, 2, 1, 3).reshape(DEPTH * 6 * N_COND, 1, D)

    bf = lambda w: w.astype(BF16)
    row = lambda v: v.reshape(1, -1)
    tf = 512

    u = _pw1_glu(xp, xs, mod, w_pw1[0], row(b_pw1[0]))
    v = _dwconv_ln_silu(u, w_dw[0], row(b_dw[0]), row(cn_g[0]), row(cn_b[0]))
    x, hm = _proj_res_ln([v], bf(w_pw2[0]), row(b_pw2[0]), [xp, xs], mod, 0,
                         row(ln1_g[0]), row(ln1_b[0]), tk=D)
    x, hm = _ffn(hm, _ffn_w1_tiles(w_ff1[0], tf), row(b_ff1[0]), bf(w_ff2[0]), row(b_ff2[0]),
                 x, mod, 0, row(ln2_g[0]), row(ln2_b[0]), last=False)

    cos_t, sin_t = _rope_tables()
    proj = _ret_in_proj(hm, w_ret_in[0], cos_t, sin_t)
    rate = jnp.broadcast_to(ret_log2_rate[0][:, :, None, None], (2, HEADS, 1, DV))
    gp, new_state = _retention(rate, proj, None, seq=SEQ, n_seq=BATCH,
                               row_block0=0, write_state=True)
    (gs,) = _retention(rate, proj, state_ret, seq=DEC_SEQ, n_seq=DEC_BATCH,
                       row_block0=P_TOK // DEC_SEQ, write_state=False)
    x, hm = _proj_res_ln([gp, gs], bf(w_ret_o[0]), jnp.zeros((1, D), F32), [x], mod, 1,
                         row(ln1_g[1]), row(ln1_b[1]), tk=D)
    yp, ys = _ffn(hm, _ffn_w1_tiles(w_ff1[1], tf), row(b_ff1[1]), bf(w_ff2[1]), row(b_ff2[1]),
                  x, mod, 1, row(ln2_g[1]), row(ln2_b[1]), last=True)

    return (yp.reshape(BATCH, SEQ, D), ys.reshape(DEC_BATCH, DEC_SEQ, D), new_state)
```

```python
import functools

import jax
import jax.numpy as jnp
from jax import lax
from jax.experimental import pallas as pl
from jax.experimental.pallas import tpu as pltpu

F32 = jnp.float32
BF16 = jnp.bfloat16

D = 2048
BATCH = 16
SEQ = 256
DEC_BATCH = 4
DEC_SEQ = 2048
GRID_W = 64
CONV_WIDTH = 31
D_FF = 4 * D
HEADS = 8
DK = D // HEADS
DV = 2 * D // HEADS
ROPE_BASE = 10000.0
LN_EPS = 1e-5
GN_EPS = 1e-5
DEPTH = 2
ALPHA = (2.0 * DEPTH) ** 0.25

P_TOK = BATCH * SEQ
S_TOK = DEC_BATCH * DEC_SEQ
N_TOK = P_TOK + S_TOK
N_COND = 8

LANES = 128
SUBLANES = 8
TM = 512
VMEM_LIMIT = 56 * 1024 * 1024


def _cparams(sem):
    return pltpu.CompilerParams(dimension_semantics=sem, vmem_limit_bytes=VMEM_LIMIT)


def _cond_index(i, tm):
    r = i * tm
    return jnp.where(r < P_TOK, 0, 1 + (r - P_TOK) // DEC_SEQ)


def _mod_spec(layer, part, tm, row_axis=0):
    base = (layer * 6 + part) * N_COND
    return pl.BlockSpec((None, 1, D),
                        lambda *ids: (base + _cond_index(ids[row_axis], tm), 0, 0))


def _const_spec(shape):
    return pl.BlockSpec(shape, lambda *ids: (0,) * len(shape))


def _group_specs(tm, cols, row_axis=0):
    n_p = P_TOK // tm
    ctx = pl.BlockSpec((tm, cols), lambda *ids: (jnp.minimum(ids[row_axis], n_p - 1), 0))
    lat = pl.BlockSpec((tm, cols), lambda *ids: (jnp.maximum(ids[row_axis] - n_p, 0), 0))
    return [ctx, lat]


def _layer_norm(r, g, b):
    mu = jnp.mean(r, axis=-1, keepdims=True)
    d = r - mu
    var = jnp.mean(d * d, axis=-1, keepdims=True)
    return d * lax.rsqrt(var + LN_EPS) * g + b


def _silu(x):
    return x * jax.nn.sigmoid(x)


def _dot(a, b):
    return jnp.dot(a, b, preferred_element_type=F32)


def _mod_kernel(cond_ref, w_ref, b_ref, o_ref):
    s = _silu(cond_ref[...]).astype(BF16)
    o_ref[...] = _dot(s, w_ref[...].astype(BF16)) + b_ref[...]


def _modulation(cond, w_mod, b_mod):
    tn = 1024
    return pl.pallas_call(
        _mod_kernel,
        grid=(DEPTH, 6 * D // tn),
        in_specs=[
            pl.BlockSpec((N_COND, D), lambda l, j: (0, 0)),
            pl.BlockSpec((None, D, tn), lambda l, j: (l, 0, j)),
            pl.BlockSpec((None, 1, tn), lambda l, j: (l, 0, j)),
        ],
        out_specs=pl.BlockSpec((None, N_COND, tn), lambda l, j: (l, 0, j)),
        out_shape=jax.ShapeDtypeStruct((DEPTH, N_COND, 6 * D), F32),
        compiler_params=_cparams(("parallel", "parallel")),
        name="modulation",
    )(cond, w_mod, b_mod.reshape(DEPTH, 1, 6 * D))


def _pw1_glu_kernel(xp_ref, xs_ref, sc_ref, sh_ref, wa_ref, wg_ref, ba_ref, bg_ref, o_ref,
                    wa_bf, wg_bf, *, n_p):
    i = pl.program_id(1)

    @pl.when(i == 0)
    def _():
        wa_bf[...] = wa_ref[...].astype(BF16)
        wg_bf[...] = wg_ref[...].astype(BF16)

    x = jnp.where(i < n_p, xp_ref[...], xs_ref[...])
    hm = (x * (1.0 + sc_ref[...]) + sh_ref[...]).astype(BF16)
    a = _dot(hm, wa_bf[...]) + ba_ref[...]
    g = _dot(hm, wg_bf[...]) + bg_ref[...]
    o_ref[...] = a * jax.nn.sigmoid(g)


def _pw1_glu(xp, xs, mod, w_pw1, b_pw1):
    tm, tn = TM, 512
    nj = D // tn
    return pl.pallas_call(
        functools.partial(_pw1_glu_kernel, n_p=P_TOK // tm),
        grid=(nj, N_TOK // tm),
        in_specs=_group_specs(tm, D, row_axis=1) + [
            _mod_spec(0, 1, tm, row_axis=1),
            _mod_spec(0, 0, tm, row_axis=1),
            pl.BlockSpec((D, tn), lambda j, i: (0, j)),
            pl.BlockSpec((D, tn), lambda j, i: (0, j + nj)),
            pl.BlockSpec((1, tn), lambda j, i: (0, j)),
            pl.BlockSpec((1, tn), lambda j, i: (0, j + nj)),
        ],
        out_specs=pl.BlockSpec((tm, tn), lambda j, i: (i, j)),
        out_shape=jax.ShapeDtypeStruct((N_TOK, D), F32),
        scratch_shapes=[pltpu.VMEM((D, tn), BF16), pltpu.VMEM((D, tn), BF16)],
        compiler_params=_cparams(("parallel", "arbitrary")),
        name="pw1_glu",
    )(xp, xs, mod, mod, w_pw1, w_pw1, b_pw1, b_pw1)


CONV_TB = 256
CONV_HALO = 16
CONV_WIN = CONV_TB + 2 * CONV_HALO
CONV_CW = 256
CONV_RB = 32


def _dwconv_kernel(prev_ref, cur_ref, next_ref, w_ref, b_ref, g_ref, beta_ref, o_ref,
                   win_ref, sh_ref, wb_ref, y_ref):
    i = pl.program_id(0)

    @pl.when(i == 0)
    def _():
        for k in range(CONV_WIDTH):
            wb_ref[k] = jnp.broadcast_to(w_ref[pl.ds(k, 1), :], (SUBLANES, D))

    n_p = P_TOK // CONV_TB
    per_seq = DEC_SEQ // CONV_TB
    pos = (i - n_p) % per_seq
    latent = i >= n_p
    has_prev = jnp.logical_and(latent, pos != 0)
    has_next = jnp.logical_and(latent, pos != per_seq - 1)
    win_ref[0:CONV_HALO, :] = jnp.where(has_prev, prev_ref[...], 0.0)
    win_ref[CONV_HALO:CONV_HALO + CONV_TB, :] = cur_ref[...]
    win_ref[CONV_HALO + CONV_TB:, :] = jnp.where(has_next, next_ref[...], 0.0)

    shift = CONV_HALO - CONV_WIDTH // 2

    def col_body(c, carry):
        cols = pl.ds(pl.multiple_of(c * CONV_CW, CONV_CW), CONV_CW)
        w0 = win_ref[:, cols]
        for s in range(1, SUBLANES):
            sh_ref[s - 1] = pltpu.roll(w0, CONV_WIN - s, axis=0)
        groups = CONV_RB // SUBLANES
        for r in range(0, CONV_TB, CONV_RB):
            accs = [jnp.zeros((SUBLANES, CONV_CW), F32)] * groups
            for k in range(CONV_WIDTH):
                off = shift + k
                s, base = off % SUBLANES, r + off - off % SUBLANES
                if s == 0:
                    src = win_ref[pl.ds(base, CONV_RB), cols]
                else:
                    src = sh_ref[s - 1, pl.ds(base, CONV_RB), :]
                wk = wb_ref[k, :, cols]
                accs = [a + src[g * SUBLANES:(g + 1) * SUBLANES] * wk
                        for g, a in enumerate(accs)]
            y_ref[pl.ds(r, CONV_RB), cols] = jnp.concatenate(accs, axis=0)
        return carry

    lax.fori_loop(0, D // CONV_CW, col_body, 0)
    y = y_ref[...] + b_ref[...]
    o_ref[...] = _silu(_layer_norm(y, g_ref[...], beta_ref[...])).astype(BF16)


def _dwconv_ln_silu(u, w_dw, b_dw, cn_g, cn_b):
    hb = CONV_TB // CONV_HALO
    n_halo_blocks = N_TOK // CONV_HALO
    vec = pl.BlockSpec((1, D), lambda i: (0, 0))
    return pl.pallas_call(
        _dwconv_kernel,
        grid=(N_TOK // CONV_TB,),
        in_specs=[
            pl.BlockSpec((CONV_HALO, D), lambda i: (jnp.maximum(i * hb - 1, 0), 0)),
            pl.BlockSpec((CONV_TB, D), lambda i: (i, 0)),
            pl.BlockSpec((CONV_HALO, D), lambda i: (jnp.minimum((i + 1) * hb, n_halo_blocks - 1), 0)),
            pl.BlockSpec((CONV_WIDTH, D), lambda i: (0, 0)),
            vec, vec, vec,
        ],
        out_specs=pl.BlockSpec((CONV_TB, D), lambda i: (i, 0)),
        out_shape=jax.ShapeDtypeStruct((N_TOK, D), BF16),
        scratch_shapes=[pltpu.VMEM((CONV_WIN, D), F32),
                        pltpu.VMEM((SUBLANES - 1, CONV_WIN, CONV_CW), F32),
                        pltpu.VMEM((CONV_WIDTH, SUBLANES, D), F32),
                        pltpu.VMEM((CONV_TB, D), F32)],
        compiler_params=_cparams(("arbitrary",)),
        name="dwconv_ln_silu",
    )(u, u, u, w_dw, b_dw, cn_g, cn_b)


EPI_SUB = 256


def _proj_res_ln_kernel(*refs, n_p, lhs_split, res_split, tm, sub):
    refs = list(refs)
    a_refs = [refs.pop(0) for _ in range(2 if lhs_split else 1)]
    w_ref, b_ref = refs.pop(0), refs.pop(0)
    x_refs = [refs.pop(0) for _ in range(2 if res_split else 1)]
    gate_ref, lng_ref, lnb_ref, sc_ref, sh_ref, xo_ref, hmo_ref = refs
    i = pl.program_id(0)

    def pick(pair, rows):
        if len(pair) == 2:
            return jnp.where(i < n_p, pair[0][rows, :], pair[1][rows, :])
        return pair[0][rows, :]

    for s in range(tm // sub):
        rows = pl.ds(s * sub, sub)
        y = _dot(pick(a_refs, rows), w_ref[...]) + b_ref[...]
        xn = _layer_norm(ALPHA * pick(x_refs, rows) + gate_ref[...] * y,
                         lng_ref[...], lnb_ref[...])
        xo_ref[rows, :] = xn
        hmo_ref[rows, :] = (xn * (1.0 + sc_ref[...]) + sh_ref[...]).astype(BF16)


def _proj_res_ln(a_list, w, b, x_list, mod, layer, ln_g, ln_b, tm, sub):
    kdim = w.shape[0]
    n_p = P_TOK // tm
    lhs_split, res_split = len(a_list) == 2, len(x_list) == 2

    def row_specs(split, cols):
        if not split:
            return [pl.BlockSpec((tm, cols), lambda i: (i, 0))]
        return _group_specs(tm, cols)

    in_specs = (row_specs(lhs_split, kdim)
                + [pl.BlockSpec((kdim, D), lambda i: (0, 0), pipeline_mode=pl.Buffered(1)),
                   _const_spec((1, D))]
                + row_specs(res_split, D)
                + [_mod_spec(layer, 2, tm), _const_spec((1, D)), _const_spec((1, D)),
                   _mod_spec(layer, 4, tm), _mod_spec(layer, 3, tm)])
    return pl.pallas_call(
        functools.partial(_proj_res_ln_kernel, n_p=n_p, lhs_split=lhs_split,
                          res_split=res_split, tm=tm, sub=sub),
        grid=(N_TOK // tm,),
        in_specs=in_specs,
        out_specs=[pl.BlockSpec((tm, D), lambda i: (i, 0)),
                   pl.BlockSpec((tm, D), lambda i: (i, 0))],
        out_shape=[jax.ShapeDtypeStruct((N_TOK, D), F32),
                   jax.ShapeDtypeStruct((N_TOK, D), BF16)],
        compiler_params=_cparams(("parallel",)),
        name=f"proj_res_ln_l{layer}",
    )(*a_list, w, b, *x_list, mod, ln_g, ln_b, mod, mod)


def _ffn_kernel(*refs, nf, n_p, last):
    hm_ref, w1_ref, b1_ref, w2_ref, b2_ref, x_ref, gate_ref, lng_ref, lnb_ref = refs[:9]
    if last:
        yp_ref, ys_ref, acc_ref = refs[9:]
    else:
        sc_ref, sh_ref, xo_ref, hmo_ref, acc_ref = refs[9:]
    i = pl.program_id(0)
    f = pl.program_id(1)
    tm = hm_ref.shape[0]

    h = _dot(hm_ref[...], w1_ref[...]) + b1_ref[...]
    h = jnp.square(jnp.maximum(h, 0.0)).astype(BF16)

    @pl.when(f == 0)
    def _():
        acc_ref[...] = _dot(h, w2_ref[...])

    @pl.when(jnp.logical_and(f > 0, f < nf - 1))
    def _():
        acc_ref[...] += _dot(h, w2_ref[...])

    def finish(store):
        for s in range(tm // EPI_SUB):
            rows = pl.ds(s * EPI_SUB, EPI_SUB)
            y = (acc_ref[rows, :] + _dot(h[s * EPI_SUB:(s + 1) * EPI_SUB], w2_ref[...])
                 + b2_ref[...])
            store(rows, _layer_norm(ALPHA * x_ref[rows, :] + gate_ref[...] * y,
                                    lng_ref[...], lnb_ref[...]))

    def store_ctx(rows, xn):
        yp_ref[rows, :] = xn

    def store_lat(rows, xn):
        ys_ref[rows, :] = xn

    def store_next(rows, xn):
        xo_ref[rows, :] = xn
        hmo_ref[rows, :] = (xn * (1.0 + sc_ref[...]) + sh_ref[...]).astype(BF16)

    is_last_f = f == nf - 1
    if last:
        pl.when(jnp.logical_and(is_last_f, i < n_p))(lambda: finish(store_ctx))
        pl.when(jnp.logical_and(is_last_f, i >= n_p))(lambda: finish(store_lat))
    else:
        pl.when(is_last_f)(lambda: finish(store_next))


def _ffn(hm, w1, b1, w2, b2, x, mod, layer, ln_g, ln_b, last):
    tm, tf = TM, 1024
    nf = D_FF // tf
    n_p = P_TOK // tm
    in_specs = [
        pl.BlockSpec((tm, D), lambda i, f: (i, 0)),
        pl.BlockSpec((D, tf), lambda i, f: (0, f)),
        pl.BlockSpec((1, tf), lambda i, f: (0, f)),
        pl.BlockSpec((tf, D), lambda i, f: (f, 0)),
        _const_spec((1, D)),
        pl.BlockSpec((tm, D), lambda i, f: (i, 0), pipeline_mode=pl.Buffered(1)),
        _mod_spec(layer, 5, tm),
        _const_spec((1, D)), _const_spec((1, D)),
    ]
    args = [hm, w1, b1, w2, b2, x, mod, ln_g, ln_b]
    if last:
        out_specs = _group_specs(tm, D)
        out_shape = [jax.ShapeDtypeStruct((P_TOK, D), F32), jax.ShapeDtypeStruct((S_TOK, D), F32)]
    else:
        in_specs += [_mod_spec(layer + 1, 1, tm), _mod_spec(layer + 1, 0, tm)]
        args += [mod, mod]
        out_specs = [pl.BlockSpec((tm, D), lambda i, f: (i, 0))] * 2
        out_shape = [jax.ShapeDtypeStruct((N_TOK, D), F32), jax.ShapeDtypeStruct((N_TOK, D), BF16)]
    return pl.pallas_call(
        functools.partial(_ffn_kernel, nf=nf, n_p=n_p, last=last),
        grid=(N_TOK // tm, nf),
        in_specs=in_specs,
        out_specs=out_specs,
        out_shape=out_shape,
        scratch_shapes=[pltpu.VMEM((tm, D), F32)],
        compiler_params=_cparams(("arbitrary" if last else "parallel", "arbitrary")),
        name=f"ffn_l{layer}",
    )(*args)


def _rope_tile(y, cos_ref, sin_ref):
    half = DK // 2
    outs = []
    for s in range(y.shape[1] // half):
        x = y[:, s * half:(s + 1) * half]
        t = (s % 2) * half
        rot = pltpu.roll(x, half // 2, axis=1)
        outs.append(x * cos_ref[:, t:t + half] + rot * sin_ref[:, t:t + half])
    return jnp.concatenate(outs, axis=1)


def _ret_in_kernel(hm_ref, w_ref, cos_ref, sin_ref, o_ref, w_bf, *, tm, tn):
    j = pl.program_id(0)
    i = pl.program_id(1)

    @pl.when(i == 0)
    def _():
        w_bf[...] = w_ref[...].astype(BF16)

    y = _dot(hm_ref[...], w_bf[...])
    y = y * jnp.where(j < D // tn, DK ** -0.5, 1.0)
    use_rope = jnp.logical_and(j < 2 * D // tn, i >= P_TOK // tm)

    @pl.when(use_rope)
    def _():
        o_ref[...] = _rope_tile(y, cos_ref, sin_ref).astype(BF16)

    @pl.when(jnp.logical_not(use_rope))
    def _():
        o_ref[...] = y.astype(BF16)


def _ret_in_proj(hm, w_in, cos_t, sin_t):
    tm, tn = 1024, 1024
    n_p = P_TOK // tm
    tpos = lambda j, i: (jnp.maximum(i - n_p, 0) % (DEC_SEQ // tm), 0)
    return pl.pallas_call(
        functools.partial(_ret_in_kernel, tm=tm, tn=tn),
        grid=(6 * D // tn, N_TOK // tm),
        in_specs=[
            pl.BlockSpec((tm, D), lambda j, i: (i, 0)),
            pl.BlockSpec((D, tn), lambda j, i: (0, j)),
            pl.BlockSpec((tm, DK), tpos),
            pl.BlockSpec((tm, DK), tpos),
        ],
        out_specs=pl.BlockSpec((tm, tn), lambda j, i: (i, j)),
        out_shape=jax.ShapeDtypeStruct((N_TOK, 6 * D), BF16),
        scratch_shapes=[pltpu.VMEM((D, tn), BF16)],
        compiler_params=_cparams(("parallel", "arbitrary")),
        name="ret_in_proj",
    )(hm, w_in, cos_t, sin_t)


RC = 256


def _scale_rows(a, d):
    return jnp.concatenate(
        [a[:, s * LANES:(s + 1) * LANES] * d for s in range(a.shape[1] // LANES)], axis=1)


def _retention_kernel(*refs, seq, sps, has_s0, write_state):
    refs = list(refs)
    rate_ref, q_ref, k_ref, v_ref, g_ref = [refs.pop(0) for _ in range(5)]
    s0_ref = refs.pop(0) if has_s0 else None
    o_ref = refs.pop(0)
    so_ref = refs.pop(0) if write_state else None
    mask_ref, din_ref, dout_ref, sb_all, s_all = refs
    n = seq // RC
    c_f = float(RC)

    lg = jnp.log1p(-jnp.exp2(rate_ref[...]))
    lgf, lgb = lg[0], lg[1]
    dchunk_f = jnp.exp(lgf * c_f)
    dchunk_b = jnp.exp(lgb * c_f)

    @pl.when(pl.program_id(1) == 0)
    def _():
        rows = lax.broadcasted_iota(jnp.int32, (RC, LANES), 0).astype(F32)
        diff = (lax.broadcasted_iota(jnp.int32, (RC, RC), 0)
                - lax.broadcasted_iota(jnp.int32, (RC, RC), 1)).astype(F32)
        mf = jnp.where(diff >= 0, jnp.exp(lgf[:, :RC] * jnp.maximum(diff, 0.0)), 0.0)
        mb = jnp.where(diff <= 0, jnp.exp(lgb[:, :RC] * jnp.maximum(-diff, 0.0)), 0.0)
        mask_ref[...] = mf + mb
        lf, lb = lgf[:, :LANES], lgb[:, :LANES]
        din_ref[0] = jnp.exp(lf * (rows + 1.0))
        din_ref[1] = jnp.exp(lb * (c_f - rows))
        dout_ref[0] = jnp.exp(lf * (c_f - 1.0 - rows))
        dout_ref[1] = jnp.exp(lb * rows)

    def loop(body, unroll):
        if n <= 2:
            for t in range(n):
                body(t)
        else:
            lax.fori_loop(0, n, lambda t, carry: (body(t), carry)[1], 0, unroll=unroll)

    for sq in range(sps):
        sb_ref, s_ref = sb_all.at[sq], s_all.at[sq]

        def chunk_rows(c, sq=sq):
            if isinstance(c, int):
                return pl.ds(sq * seq + c * RC, RC)
            return pl.ds(pl.multiple_of(sq * seq + c * RC, RC), RC)

        def init_state(direction, sq=sq, s_ref=s_ref):
            if has_s0:
                s_ref[...] = s0_ref[sq, direction]
            else:
                s_ref[...] = jnp.zeros((DK, DV), F32)

        def state_update(c, direction, dchunk, chunk_rows=chunk_rows, s_ref=s_ref):
            rows = chunk_rows(c)
            kd = _scale_rows(k_ref[rows, :].astype(F32), dout_ref[direction]).astype(BF16)
            upd = lax.dot_general(kd, v_ref[rows, :], (((0,), (0,)), ((), ())),
                                  preferred_element_type=F32)
            s_ref[...] = dchunk * s_ref[...] + upd

        init_state(1)

        def pass1(t, sb_ref=sb_ref, s_ref=s_ref, state_update=state_update):
            c = n - 1 - t
            sb_ref[c] = s_ref[...].astype(BF16)
            state_update(c, 1, dchunk_b)

        loop(pass1, 1)
        if write_state:
            so_ref[sq, 1] = s_ref[...]

        init_state(0)

        def pass2(c, sb_ref=sb_ref, s_ref=s_ref, chunk_rows=chunk_rows,
                  state_update=state_update):
            rows = chunk_rows(c)
            qc = q_ref[rows, :]
            scores = lax.dot_general(qc, k_ref[rows, :], (((1,), (1,)), ((), ())),
                                     preferred_element_type=F32)
            tot = _dot((scores * mask_ref[...]).astype(BF16), v_ref[rows, :])
            if has_s0 or not (isinstance(c, int) and c == 0):
                tot = tot + _scale_rows(_dot(qc, s_ref[...].astype(BF16)), din_ref[0])
            if has_s0 or not (isinstance(c, int) and c == n - 1):
                tot = tot + _scale_rows(_dot(qc, sb_ref[c]), din_ref[1])
            mu = jnp.mean(tot, axis=-1, keepdims=True)
            d = tot - mu
            var = jnp.mean(d * d, axis=-1, keepdims=True)
            nrm = d * lax.rsqrt(var + GN_EPS)
            o_ref[rows, :] = (_silu(g_ref[rows, :].astype(F32)) * nrm).astype(BF16)
            state_update(c, 0, dchunk_f)

        loop(pass2, 2)
        if write_state:
            so_ref[sq, 0] = s_ref[...]


def _retention(rate, proj, s0, *, seq, n_seq, sps, row_block0, write_state):
    has_s0 = s0 is not None
    koff, voff, goff = D // DK, 2 * D // DV, 4 * D // DV
    rows = sps * seq
    state_block = (sps, None, 2, None, DK, DV)
    in_specs = [
        pl.BlockSpec((2, None, 1, DV), lambda h, b: (0, h, 0, 0)),
        pl.BlockSpec((rows, DK), lambda h, b: (row_block0 + b, h)),
        pl.BlockSpec((rows, DK), lambda h, b: (row_block0 + b, koff + h)),
        pl.BlockSpec((rows, DV), lambda h, b: (row_block0 + b, voff + h)),
        pl.BlockSpec((rows, DV), lambda h, b: (row_block0 + b, goff + h)),
    ]
    args = [rate, proj, proj, proj, proj]
    if has_s0:
        in_specs.append(pl.BlockSpec(state_block, lambda h, b: (b, 0, 0, h, 0, 0)))
        args.append(s0)
    out_specs = [pl.BlockSpec((rows, DV), lambda h, b: (b, h))]
    out_shape = [jax.ShapeDtypeStruct((n_seq * seq, 2 * D), BF16)]
    if write_state:
        out_specs.append(pl.BlockSpec(state_block, lambda h, b: (b, 0, 0, h, 0, 0)))
        out_shape.append(jax.ShapeDtypeStruct((n_seq, 1, 2, HEADS, DK, DV), F32))
    return pl.pallas_call(
        functools.partial(_retention_kernel, seq=seq, sps=sps, has_s0=has_s0,
                          write_state=write_state),
        grid=(HEADS, n_seq // sps),
        in_specs=in_specs,
        out_specs=out_specs,
        out_shape=out_shape,
        scratch_shapes=[pltpu.VMEM((RC, RC), F32),
                        pltpu.VMEM((2, RC, LANES), F32),
                        pltpu.VMEM((2, RC, LANES), F32),
                        pltpu.VMEM((sps, seq // RC, DK, DV), BF16),
                        pltpu.VMEM((sps, DK, DV), F32)],
        compiler_params=_cparams(("parallel", "arbitrary")),
        name=f"retention_t{seq}",
    )(*args)


def _rope_tables():
    t = jnp.arange(DEC_SEQ)
    row = (t // GRID_W).astype(F32)
    col = (t % GRID_W).astype(F32)
    quarter = DK // 4
    inv = ROPE_BASE ** (-jnp.arange(quarter, dtype=F32) / quarter)
    ang_r = row[:, None] * inv
    ang_c = col[:, None] * inv
    cos_t = jnp.concatenate([jnp.cos(ang_r)] * 2 + [jnp.cos(ang_c)] * 2, axis=1)
    sin_t = jnp.concatenate([-jnp.sin(ang_r), jnp.sin(ang_r), -jnp.sin(ang_c), jnp.sin(ang_c)], axis=1)
    return cos_t, sin_t


def kernel(x_prompt, x_sample, state_ret, c, c_ctx, w_mod, b_mod, ln1_g, ln1_b, ln2_g, ln2_b,
           w_pw1, b_pw1, w_dw, b_dw, cn_g, cn_b, w_pw2, b_pw2,
           w_ret_in, ret_log2_rate, w_ret_o, w_ff1, b_ff1, w_ff2, b_ff2):
    xp = x_prompt.reshape(P_TOK, D)
    xs = x_sample.reshape(S_TOK, D)
    cond = jnp.concatenate(
        [c_ctx[None, :], c, jnp.zeros((N_COND - 1 - DEC_BATCH, D), F32)], axis=0)

    mod = _modulation(cond, w_mod, b_mod)
    mod = mod.reshape(DEPTH, N_COND, 6, D).transpose(0, 2, 1, 3).reshape(DEPTH * 6 * N_COND, 1, D)

    bf = lambda w: w.astype(BF16)
    row = lambda v: v.reshape(1, -1)

    u = _pw1_glu(xp, xs, mod, w_pw1[0], row(b_pw1[0]))
    v = _dwconv_ln_silu(u, w_dw[0], row(b_dw[0]), row(cn_g[0]), row(cn_b[0]))
    x, hm = _proj_res_ln([v], bf(w_pw2[0]), row(b_pw2[0]), [xp, xs], mod, 0,
                         row(ln1_g[0]), row(ln1_b[0]), tm=512, sub=256)
    x, hm = _ffn(hm, bf(w_ff1[0]), row(b_ff1[0]), bf(w_ff2[0]), row(b_ff2[0]),
                 x, mod, 0, row(ln2_g[0]), row(ln2_b[0]), last=False)

    cos_t, sin_t = _rope_tables()
    proj = _ret_in_proj(hm, w_ret_in[0], cos_t, sin_t)
    rate = jnp.broadcast_to(ret_log2_rate[0][:, :, None, None], (2, HEADS, 1, DV))
    gp, new_state = _retention(rate, proj, None, seq=SEQ, n_seq=BATCH, sps=4,
                               row_block0=0, write_state=True)
    (gs,) = _retention(rate, proj, state_ret, seq=DEC_SEQ, n_seq=DEC_BATCH, sps=1,
                       row_block0=P_TOK // DEC_SEQ, write_state=False)
    x, hm = _proj_res_ln([gp, gs], bf(w_ret_o[0]), jnp.zeros((1, D), F32), [x], mod, 1,
                         row(ln1_g[1]), row(ln1_b[1]), tm=256, sub=128)
    yp, ys = _ffn(hm, bf(w_ff1[1]), row(b_ff1[1]), bf(w_ff2[1]), row(b_ff2[1]),
                  x, mod, 1, row(ln2_g[1]), row(ln2_b[1]), last=True)

    return (yp.reshape(BATCH, SEQ, D), ys.reshape(DEC_BATCH, DEC_SEQ, D), new_state)
```

```python
import functools

import jax
import jax.numpy as jnp
from jax import lax
from jax.experimental import pallas as pl
from jax.experimental.pallas import tpu as pltpu

F32 = jnp.float32
BF16 = jnp.bfloat16

D = 2048
BATCH = 16
SEQ = 256
DEC_BATCH = 4
DEC_SEQ = 2048
GRID_W = 64
CONV_WIDTH = 31
D_FF = 4 * D
HEADS = 8
DK = D // HEADS
DV = 2 * D // HEADS
ROPE_BASE = 10000.0
LN_EPS = 1e-5
GN_EPS = 1e-5
DEPTH = 2
ALPHA = (2.0 * DEPTH) ** 0.25

P_TOK = BATCH * SEQ
S_TOK = DEC_BATCH * DEC_SEQ
N_TOK = P_TOK + S_TOK
N_COND = 8

LANES = 128
SUBLANES = 8
TM = 512
VMEM_LIMIT = 60 * 1024 * 1024


def _cparams(sem):
    return pltpu.CompilerParams(dimension_semantics=sem, vmem_limit_bytes=VMEM_LIMIT)


def _cond_index(i, tm):
    r = i * tm
    return jnp.where(r < P_TOK, 0, 1 + (r - P_TOK) // DEC_SEQ)


def _mod_spec(layer, part, tm, row_axis=0):
    base = (layer * 6 + part) * N_COND
    return pl.BlockSpec((None, 1, D),
                        lambda *ids: (base + _cond_index(ids[row_axis], tm), 0, 0))


def _const_spec(shape):
    return pl.BlockSpec(shape, lambda *ids: (0,) * len(shape))


def _group_specs(tm, cols, row_axis=0):
    n_p = P_TOK // tm
    ctx = pl.BlockSpec((tm, cols), lambda *ids: (jnp.minimum(ids[row_axis], n_p - 1), 0))
    lat = pl.BlockSpec((tm, cols), lambda *ids: (jnp.maximum(ids[row_axis] - n_p, 0), 0))
    return [ctx, lat]


def _layer_norm(r, g, b):
    mu = jnp.mean(r, axis=-1, keepdims=True)
    d = r - mu
    var = jnp.mean(d * d, axis=-1, keepdims=True)
    return d * lax.rsqrt(var + LN_EPS) * g + b


def _silu(x):
    return x * jax.nn.sigmoid(x)


def _dot(a, b):
    return jnp.dot(a, b, preferred_element_type=F32)


def _mod_kernel(cond_ref, w_ref, b_ref, o_ref):
    s = _silu(cond_ref[...]).astype(BF16)
    o_ref[...] = _dot(s, w_ref[...].astype(BF16)) + b_ref[...]


def _modulation(cond, w_mod, b_mod):
    tn = 1024
    return pl.pallas_call(
        _mod_kernel,
        grid=(DEPTH, 6 * D // tn),
        in_specs=[
            pl.BlockSpec((N_COND, D), lambda l, j: (0, 0)),
            pl.BlockSpec((None, D, tn), lambda l, j: (l, 0, j)),
            pl.BlockSpec((None, 1, tn), lambda l, j: (l, 0, j)),
        ],
        out_specs=pl.BlockSpec((None, N_COND, tn), lambda l, j: (l, 0, j)),
        out_shape=jax.ShapeDtypeStruct((DEPTH, N_COND, 6 * D), F32),
        compiler_params=_cparams(("parallel", "parallel")),
        name="modulation",
    )(cond, w_mod, b_mod.reshape(DEPTH, 1, 6 * D))


def _pw1_glu_kernel(xp_ref, xs_ref, sc_ref, sh_ref, wa_ref, wg_ref, ba_ref, bg_ref, o_ref,
                    wa_bf, wg_bf, *, n_p):
    i = pl.program_id(1)

    @pl.when(i == 0)
    def _():
        wa_bf[...] = wa_ref[...].astype(BF16)
        wg_bf[...] = wg_ref[...].astype(BF16)

    x = jnp.where(i < n_p, xp_ref[...], xs_ref[...])
    hm = (x * (1.0 + sc_ref[...]) + sh_ref[...]).astype(BF16)
    a = _dot(hm, wa_bf[...]) + ba_ref[...]
    g = _dot(hm, wg_bf[...]) + bg_ref[...]
    o_ref[...] = a * jax.nn.sigmoid(g)


def _pw1_glu(xp, xs, mod, w_pw1, b_pw1):
    tm, tn = TM, 512
    nj = D // tn
    return pl.pallas_call(
        functools.partial(_pw1_glu_kernel, n_p=P_TOK // tm),
        grid=(nj, N_TOK // tm),
        in_specs=_group_specs(tm, D, row_axis=1) + [
            _mod_spec(0, 1, tm, row_axis=1),
            _mod_spec(0, 0, tm, row_axis=1),
            pl.BlockSpec((D, tn), lambda j, i: (0, j)),
            pl.BlockSpec((D, tn), lambda j, i: (0, j + nj)),
            pl.BlockSpec((1, tn), lambda j, i: (0, j)),
            pl.BlockSpec((1, tn), lambda j, i: (0, j + nj)),
        ],
        out_specs=pl.BlockSpec((tm, tn), lambda j, i: (i, j)),
        out_shape=jax.ShapeDtypeStruct((N_TOK, D), F32),
        scratch_shapes=[pltpu.VMEM((D, tn), BF16), pltpu.VMEM((D, tn), BF16)],
        compiler_params=_cparams(("parallel", "arbitrary")),
        name="pw1_glu",
    )(xp, xs, mod, mod, w_pw1, w_pw1, b_pw1, b_pw1)


CONV_TB = 256
CONV_HALO = 16
CONV_WIN = CONV_TB + 2 * CONV_HALO
CONV_CW = 256
CONV_RB = 32


def _dwconv_kernel(prev_ref, cur_ref, next_ref, w_ref, b_ref, g_ref, beta_ref, o_ref,
                   win_ref, sh_ref, wb_ref, y_ref):
    i = pl.program_id(0)

    @pl.when(i == 0)
    def _():
        for k in range(CONV_WIDTH):
            wb_ref[k] = jnp.broadcast_to(w_ref[pl.ds(k, 1), :], (SUBLANES, D))

    n_p = P_TOK // CONV_TB
    per_seq = DEC_SEQ // CONV_TB
    pos = (i - n_p) % per_seq
    latent = i >= n_p
    has_prev = jnp.logical_and(latent, pos != 0)
    has_next = jnp.logical_and(latent, pos != per_seq - 1)
    win_ref[0:CONV_HALO, :] = jnp.where(has_prev, prev_ref[...], 0.0)
    win_ref[CONV_HALO:CONV_HALO + CONV_TB, :] = cur_ref[...]
    win_ref[CONV_HALO + CONV_TB:, :] = jnp.where(has_next, next_ref[...], 0.0)

    shift = CONV_HALO - CONV_WIDTH // 2

    def col_body(c, carry):
        cols = pl.ds(pl.multiple_of(c * CONV_CW, CONV_CW), CONV_CW)
        w0 = win_ref[:, cols]
        for s in range(1, SUBLANES):
            sh_ref[s - 1] = pltpu.roll(w0, CONV_WIN - s, axis=0)
        groups = CONV_RB // SUBLANES
        for r in range(0, CONV_TB, CONV_RB):
            accs = [jnp.zeros((SUBLANES, CONV_CW), F32)] * groups
            for k in range(CONV_WIDTH):
                off = shift + k
                s, base = off % SUBLANES, r + off - off % SUBLANES
                if s == 0:
                    src = win_ref[pl.ds(base, CONV_RB), cols]
                else:
                    src = sh_ref[s - 1, pl.ds(base, CONV_RB), :]
                wk = wb_ref[k, :, cols]
                accs = [a + src[g * SUBLANES:(g + 1) * SUBLANES] * wk
                        for g, a in enumerate(accs)]
            y_ref[pl.ds(r, CONV_RB), cols] = jnp.concatenate(accs, axis=0)
        return carry

    lax.fori_loop(0, D // CONV_CW, col_body, 0)
    y = y_ref[...] + b_ref[...]
    o_ref[...] = _silu(_layer_norm(y, g_ref[...], beta_ref[...])).astype(BF16)


def _dwconv_ln_silu(u, w_dw, b_dw, cn_g, cn_b):
    hb = CONV_TB // CONV_HALO
    n_halo_blocks = N_TOK // CONV_HALO
    vec = pl.BlockSpec((1, D), lambda i: (0, 0))
    return pl.pallas_call(
        _dwconv_kernel,
        grid=(N_TOK // CONV_TB,),
        in_specs=[
            pl.BlockSpec((CONV_HALO, D), lambda i: (jnp.maximum(i * hb - 1, 0), 0)),
            pl.BlockSpec((CONV_TB, D), lambda i: (i, 0)),
            pl.BlockSpec((CONV_HALO, D), lambda i: (jnp.minimum((i + 1) * hb, n_halo_blocks - 1), 0)),
            pl.BlockSpec((CONV_WIDTH, D), lambda i: (0, 0)),
            vec, vec, vec,
        ],
        out_specs=pl.BlockSpec((CONV_TB, D), lambda i: (i, 0)),
        out_shape=jax.ShapeDtypeStruct((N_TOK, D), BF16),
        scratch_shapes=[pltpu.VMEM((CONV_WIN, D), F32),
                        pltpu.VMEM((SUBLANES - 1, CONV_WIN, CONV_CW), F32),
                        pltpu.VMEM((CONV_WIDTH, SUBLANES, D), F32),
                        pltpu.VMEM((CONV_TB, D), F32)],
        compiler_params=_cparams(("arbitrary",)),
        name="dwconv_ln_silu",
    )(u, u, u, w_dw, b_dw, cn_g, cn_b)


EPI_SUB = 256


def _proj_res_ln_kernel(*refs, n_p, lhs_split, res_split, tm, sub):
    refs = list(refs)
    a_refs = [refs.pop(0) for _ in range(2 if lhs_split else 1)]
    w_ref, b_ref = refs.pop(0), refs.pop(0)
    x_refs = [refs.pop(0) for _ in range(2 if res_split else 1)]
    gate_ref, lng_ref, lnb_ref, sc_ref, sh_ref, xo_ref, hmo_ref = refs
    i = pl.program_id(0)

    def pick(pair, rows):
        if len(pair) == 2:
            return jnp.where(i < n_p, pair[0][rows, :], pair[1][rows, :])
        return pair[0][rows, :]

    for s in range(tm // sub):
        rows = pl.ds(s * sub, sub)
        y = _dot(pick(a_refs, rows), w_ref[...]) + b_ref[...]
        xn = _layer_norm(ALPHA * pick(x_refs, rows) + gate_ref[...] * y,
                         lng_ref[...], lnb_ref[...])
        xo_ref[rows, :] = xn
        hmo_ref[rows, :] = (xn * (1.0 + sc_ref[...]) + sh_ref[...]).astype(BF16)


def _proj_res_ln(a_list, w, b, x_list, mod, layer, ln_g, ln_b, tm, sub):
    kdim = w.shape[0]
    n_p = P_TOK // tm
    lhs_split, res_split = len(a_list) == 2, len(x_list) == 2

    def row_specs(split, cols):
        if not split:
            return [pl.BlockSpec((tm, cols), lambda i: (i, 0))]
        return _group_specs(tm, cols)

    in_specs = (row_specs(lhs_split, kdim)
                + [pl.BlockSpec((kdim, D), lambda i: (0, 0), pipeline_mode=pl.Buffered(1)),
                   _const_spec((1, D))]
                + row_specs(res_split, D)
                + [_mod_spec(layer, 2, tm), _const_spec((1, D)), _const_spec((1, D)),
                   _mod_spec(layer, 4, tm), _mod_spec(layer, 3, tm)])
    return pl.pallas_call(
        functools.partial(_proj_res_ln_kernel, n_p=n_p, lhs_split=lhs_split,
                          res_split=res_split, tm=tm, sub=sub),
        grid=(N_TOK // tm,),
        in_specs=in_specs,
        out_specs=[pl.BlockSpec((tm, D), lambda i: (i, 0)),
                   pl.BlockSpec((tm, D), lambda i: (i, 0))],
        out_shape=[jax.ShapeDtypeStruct((N_TOK, D), F32),
                   jax.ShapeDtypeStruct((N_TOK, D), BF16)],
        compiler_params=_cparams(("parallel",)),
        name=f"proj_res_ln_l{layer}",
    )(*a_list, w, b, *x_list, mod, ln_g, ln_b, mod, mod)


def _ffn_kernel(*refs, nf, n_p, last):
    hm_ref, w1_ref, b1_ref, w2_ref, b2_ref, x_ref, gate_ref, lng_ref, lnb_ref = refs[:9]
    if last:
        yp_ref, ys_ref, acc_ref = refs[9:]
    else:
        sc_ref, sh_ref, xo_ref, hmo_ref, acc_ref = refs[9:]
    i = pl.program_id(0)
    f = pl.program_id(1)
    tm = hm_ref.shape[0]

    h = _dot(hm_ref[...], w1_ref[...]) + b1_ref[...]
    h = jnp.square(jnp.maximum(h, 0.0)).astype(BF16)

    @pl.when(f == 0)
    def _():
        acc_ref[...] = _dot(h, w2_ref[...])

    @pl.when(jnp.logical_and(f > 0, f < nf - 1))
    def _():
        acc_ref[...] += _dot(h, w2_ref[...])

    def finish(store):
        for s in range(tm // EPI_SUB):
            rows = pl.ds(s * EPI_SUB, EPI_SUB)
            y = (acc_ref[rows, :] + _dot(h[s * EPI_SUB:(s + 1) * EPI_SUB], w2_ref[...])
                 + b2_ref[...])
            store(rows, _layer_norm(ALPHA * x_ref[rows, :] + gate_ref[...] * y,
                                    lng_ref[...], lnb_ref[...]))

    def store_ctx(rows, xn):
        yp_ref[rows, :] = xn

    def store_lat(rows, xn):
        ys_ref[rows, :] = xn

    def store_next(rows, xn):
        xo_ref[rows, :] = xn
        hmo_ref[rows, :] = (xn * (1.0 + sc_ref[...]) + sh_ref[...]).astype(BF16)

    is_last_f = f == nf - 1
    if last:
        pl.when(jnp.logical_and(is_last_f, i < n_p))(lambda: finish(store_ctx))
        pl.when(jnp.logical_and(is_last_f, i >= n_p))(lambda: finish(store_lat))
    else:
        pl.when(is_last_f)(lambda: finish(store_next))


FFN_TF = 1024


def _w1_tiles_kernel(w_ref, o_ref):
    o_ref[...] = w_ref[...].astype(BF16)


def _ffn_w1_tiles(w_ff1):
    nf = D_FF // FFN_TF
    return pl.pallas_call(
        _w1_tiles_kernel,
        grid=(DEPTH, nf),
        in_specs=[pl.BlockSpec((None, D, FFN_TF), lambda l, f: (l, 0, f))],
        out_specs=pl.BlockSpec((None, None, D, FFN_TF), lambda l, f: (l, f, 0, 0)),
        out_shape=jax.ShapeDtypeStruct((DEPTH, nf, D, FFN_TF), BF16),
        compiler_params=_cparams(("parallel", "parallel")),
        name="ffn_w1_tiles",
    )(w_ff1)


def _ffn(hm, w1_tiles, b1, w2, b2, x, mod, layer, ln_g, ln_b, last):
    tm, tf = TM, FFN_TF
    nf = D_FF // tf
    n_p = P_TOK // tm
    in_specs = [
        pl.BlockSpec((tm, D), lambda i, f: (i, 0)),
        pl.BlockSpec((None, None, D, tf), lambda i, f: (layer, f, 0, 0)),
        pl.BlockSpec((1, tf), lambda i, f: (0, f)),
        pl.BlockSpec((None, tf, D), lambda i, f: (layer, f, 0)),
        _const_spec((1, D)),
        pl.BlockSpec((tm, D), lambda i, f: (i, 0)),
        _mod_spec(layer, 5, tm),
        _const_spec((1, D)), _const_spec((1, D)),
    ]
    args = [hm, w1_tiles, b1, w2, b2, x, mod, ln_g, ln_b]
    if last:
        out_specs = _group_specs(tm, D)
        out_shape = [jax.ShapeDtypeStruct((P_TOK, D), F32), jax.ShapeDtypeStruct((S_TOK, D), F32)]
    else:
        in_specs += [_mod_spec(layer + 1, 1, tm), _mod_spec(layer + 1, 0, tm)]
        args += [mod, mod]
        out_specs = [pl.BlockSpec((tm, D), lambda i, f: (i, 0))] * 2
        out_shape = [jax.ShapeDtypeStruct((N_TOK, D), F32), jax.ShapeDtypeStruct((N_TOK, D), BF16)]
    return pl.pallas_call(
        functools.partial(_ffn_kernel, nf=nf, n_p=n_p, last=last),
        grid=(N_TOK // tm, nf),
        in_specs=in_specs,
        out_specs=out_specs,
        out_shape=out_shape,
        scratch_shapes=[pltpu.VMEM((tm, D), F32)],
        compiler_params=_cparams(("arbitrary" if last else "parallel", "arbitrary")),
        name=f"ffn_l{layer}",
    )(*args)


def _rope_tile(y, cos_ref, sin_ref):
    half = DK // 2
    outs = []
    for s in range(y.shape[1] // half):
        x = y[:, s * half:(s + 1) * half]
        t = (s % 2) * half
        rot = pltpu.roll(x, half // 2, axis=1)
        outs.append(x * cos_ref[:, t:t + half] + rot * sin_ref[:, t:t + half])
    return jnp.concatenate(outs, axis=1)


def _ret_in_kernel(hm_ref, w_ref, cos_ref, sin_ref, o_ref, w_bf, *, tm, tn):
    j = pl.program_id(0)
    i = pl.program_id(1)

    @pl.when(i == 0)
    def _():
        w_bf[...] = w_ref[...].astype(BF16)

    y = _dot(hm_ref[...], w_bf[...])
    y = y * jnp.where(j < D // tn, DK ** -0.5, 1.0)
    use_rope = jnp.logical_and(j < 2 * D // tn, i >= P_TOK // tm)

    def store(t):
        for s in range(tn // DK):
            o_ref[s] = t[:, s * DK:(s + 1) * DK].astype(BF16)

    @pl.when(use_rope)
    def _():
        store(_rope_tile(y, cos_ref, sin_ref))

    @pl.when(jnp.logical_not(use_rope))
    def _():
        store(y)


def _ret_in_proj(hm, w_in, cos_t, sin_t):
    tm, tn = 1024, 1024
    n_p = P_TOK // tm
    tpos = lambda j, i: (jnp.maximum(i - n_p, 0) % (DEC_SEQ // tm), 0)
    return pl.pallas_call(
        functools.partial(_ret_in_kernel, tm=tm, tn=tn),
        grid=(6 * D // tn, N_TOK // tm),
        in_specs=[
            pl.BlockSpec((tm, D), lambda j, i: (i, 0)),
            pl.BlockSpec((D, tn), lambda j, i: (0, j)),
            pl.BlockSpec((tm, DK), tpos),
            pl.BlockSpec((tm, DK), tpos),
        ],
        out_specs=pl.BlockSpec((tn // DK, tm, DK), lambda j, i: (j, i, 0)),
        out_shape=jax.ShapeDtypeStruct((6 * D // DK, N_TOK, DK), BF16),
        scratch_shapes=[pltpu.VMEM((D, tn), BF16)],
        compiler_params=_cparams(("parallel", "arbitrary")),
        name="ret_in_proj",
    )(hm, w_in, cos_t, sin_t)


RC = 256


def _scale_rows(a, d):
    return jnp.concatenate(
        [a[:, s * LANES:(s + 1) * LANES] * d for s in range(a.shape[1] // LANES)], axis=1)


def _two_slabs(ref, rows):
    return jnp.concatenate([ref[0, rows, :], ref[1, rows, :]], axis=1)


def _retention_kernel(*refs, seq, sps, has_s0, write_state):
    refs = list(refs)
    rate_ref, q_ref, k_ref, v_ref, g_ref = [refs.pop(0) for _ in range(5)]
    s0_ref = refs.pop(0) if has_s0 else None
    o_ref = refs.pop(0)
    so_ref = refs.pop(0) if write_state else None
    mask_ref, din_ref, dout_ref, sb_all, s_all = refs
    n = seq // RC
    c_f = float(RC)

    lg = jnp.log1p(-jnp.exp2(rate_ref[...]))
    lgf, lgb = lg[0], lg[1]
    dchunk_f = jnp.exp(lgf * c_f)
    dchunk_b = jnp.exp(lgb * c_f)

    @pl.when(pl.program_id(1) == 0)
    def _():
        rows = lax.broadcasted_iota(jnp.int32, (RC, LANES), 0).astype(F32)
        diff = (lax.broadcasted_iota(jnp.int32, (RC, RC), 0)
                - lax.broadcasted_iota(jnp.int32, (RC, RC), 1)).astype(F32)
        mf = jnp.where(diff >= 0, jnp.exp(lgf[:, :RC] * jnp.maximum(diff, 0.0)), 0.0)
        mb = jnp.where(diff <= 0, jnp.exp(lgb[:, :RC] * jnp.maximum(-diff, 0.0)), 0.0)
        mask_ref[...] = mf + mb
        lf, lb = lgf[:, :LANES], lgb[:, :LANES]
        din_ref[0] = jnp.exp(lf * (rows + 1.0))
        din_ref[1] = jnp.exp(lb * (c_f - rows))
        dout_ref[0] = jnp.exp(lf * (c_f - 1.0 - rows))
        dout_ref[1] = jnp.exp(lb * rows)

    def loop(body, unroll):
        if n <= 2:
            for t in range(n):
                body(t)
        else:
            lax.fori_loop(0, n, lambda t, carry: (body(t), carry)[1], 0, unroll=unroll)

    for sq in range(sps):
        sb_ref, s_ref = sb_all.at[sq], s_all.at[sq]

        def chunk_rows(c, sq=sq):
            if isinstance(c, int):
                return pl.ds(sq * seq + c * RC, RC)
            return pl.ds(pl.multiple_of(sq * seq + c * RC, RC), RC)

        def init_state(direction, sq=sq, s_ref=s_ref):
            if has_s0:
                s_ref[...] = s0_ref[sq, direction]
            else:
                s_ref[...] = jnp.zeros((DK, DV), F32)

        def state_update(c, direction, dchunk, chunk_rows=chunk_rows, s_ref=s_ref):
            rows = chunk_rows(c)
            kd = _scale_rows(k_ref[rows, :].astype(F32), dout_ref[direction]).astype(BF16)
            upd = lax.dot_general(kd, _two_slabs(v_ref, rows), (((0,), (0,)), ((), ())),
                                  preferred_element_type=F32)
            s_ref[...] = dchunk * s_ref[...] + upd

        init_state(1)

        def pass1(t, sb_ref=sb_ref, s_ref=s_ref, state_update=state_update):
            c = n - 1 - t
            sb_ref[c] = s_ref[...].astype(BF16)
            state_update(c, 1, dchunk_b)

        loop(pass1, 1)
        if write_state:
            so_ref[sq, 1] = s_ref[...]

        init_state(0)

        def pass2(c, sb_ref=sb_ref, s_ref=s_ref, chunk_rows=chunk_rows,
                  state_update=state_update):
            rows = chunk_rows(c)
            qc = q_ref[rows, :]
            scores = lax.dot_general(qc, k_ref[rows, :], (((1,), (1,)), ((), ())),
                                     preferred_element_type=F32)
            tot = _dot((scores * mask_ref[...]).astype(BF16), _two_slabs(v_ref, rows))
            if has_s0 or not (isinstance(c, int) and c == 0):
                tot = tot + _scale_rows(_dot(qc, s_ref[...].astype(BF16)), din_ref[0])
            if has_s0 or not (isinstance(c, int) and c == n - 1):
                tot = tot + _scale_rows(_dot(qc, sb_ref[c]), din_ref[1])
            mu = jnp.mean(tot, axis=-1, keepdims=True)
            d = tot - mu
            var = jnp.mean(d * d, axis=-1, keepdims=True)
            nrm = d * lax.rsqrt(var + GN_EPS)
            gate = _silu(_two_slabs(g_ref, rows).astype(F32))
            o_ref[rows, :] = (gate * nrm).astype(BF16)
            state_update(c, 0, dchunk_f)

        loop(pass2, 2)
        if write_state:
            so_ref[sq, 0] = s_ref[...]


def _retention(rate, proj, s0, *, seq, n_seq, sps, row_block0, write_state):
    has_s0 = s0 is not None
    koff, voff, goff = D // DK, 2 * D // DV, 4 * D // DV
    rows = sps * seq
    state_block = (sps, None, 2, None, DK, DV)
    in_specs = [
        pl.BlockSpec((2, None, 1, DV), lambda h, b: (0, h, 0, 0)),
        pl.BlockSpec((None, rows, DK), lambda h, b: (h, row_block0 + b, 0)),
        pl.BlockSpec((None, rows, DK), lambda h, b: (koff + h, row_block0 + b, 0)),
        pl.BlockSpec((2, rows, DK), lambda h, b: (voff + h, row_block0 + b, 0)),
        pl.BlockSpec((2, rows, DK), lambda h, b: (goff + h, row_block0 + b, 0)),
    ]
    args = [rate, proj, proj, proj, proj]
    if has_s0:
        in_specs.append(pl.BlockSpec(state_block, lambda h, b: (b, 0, 0, h, 0, 0)))
        args.append(s0)
    out_specs = [pl.BlockSpec((rows, DV), lambda h, b: (b, h))]
    out_shape = [jax.ShapeDtypeStruct((n_seq * seq, 2 * D), BF16)]
    if write_state:
        out_specs.append(pl.BlockSpec(state_block, lambda h, b: (b, 0, 0, h, 0, 0)))
        out_shape.append(jax.ShapeDtypeStruct((n_seq, 1, 2, HEADS, DK, DV), F32))
    return pl.pallas_call(
        functools.partial(_retention_kernel, seq=seq, sps=sps, has_s0=has_s0,
                          write_state=write_state),
        grid=(HEADS, n_seq // sps),
        in_specs=in_specs,
        out_specs=out_specs,
        out_shape=out_shape,
        scratch_shapes=[pltpu.VMEM((RC, RC), F32),
                        pltpu.VMEM((2, RC, LANES), F32),
                        pltpu.VMEM((2, RC, LANES), F32),
                        pltpu.VMEM((sps, seq // RC, DK, DV), BF16),
                        pltpu.VMEM((sps, DK, DV), F32)],
        compiler_params=_cparams(("parallel", "arbitrary")),
        name=f"retention_t{seq}",
    )(*args)


def _rope_tables():
    t = jnp.arange(DEC_SEQ)
    row = (t // GRID_W).astype(F32)
    col = (t % GRID_W).astype(F32)
    quarter = DK // 4
    inv = ROPE_BASE ** (-jnp.arange(quarter, dtype=F32) / quarter)
    ang_r = row[:, None] * inv
    ang_c = col[:, None] * inv
    cos_t = jnp.concatenate([jnp.cos(ang_r)] * 2 + [jnp.cos(ang_c)] * 2, axis=1)
    sin_t = jnp.concatenate([-jnp.sin(ang_r), jnp.sin(ang_r), -jnp.sin(ang_c), jnp.sin(ang_c)], axis=1)
    return cos_t, sin_t


def kernel(x_prompt, x_sample, state_ret, c, c_ctx, w_mod, b_mod, ln1_g, ln1_b, ln2_g, ln2_b,
           w_pw1, b_pw1, w_dw, b_dw, cn_g, cn_b, w_pw2, b_pw2,
           w_ret_in, ret_log2_rate, w_ret_o, w_ff1, b_ff1, w_ff2, b_ff2):
    xp = x_prompt.reshape(P_TOK, D)
    xs = x_sample.reshape(S_TOK, D)
    cond = jnp.concatenate(
        [c_ctx[None, :], c, jnp.zeros((N_COND - 1 - DEC_BATCH, D), F32)], axis=0)

    mod = _modulation(cond, w_mod, b_mod)
    mod = mod.reshape(DEPTH, N_COND, 6, D).transpose(0, 2, 1, 3).reshape(DEPTH * 6 * N_COND, 1, D)

    bf = lambda w: w.astype(BF16)
    row = lambda v: v.reshape(1, -1)

    u = _pw1_glu(xp, xs, mod, w_pw1[0], row(b_pw1[0]))
    v = _dwconv_ln_silu(u, w_dw[0], row(b_dw[0]), row(cn_g[0]), row(cn_b[0]))
    x, hm = _proj_res_ln([v], bf(w_pw2[0]), row(b_pw2[0]), [xp, xs], mod, 0,
                         row(ln1_g[0]), row(ln1_b[0]), tm=512, sub=256)
    w1_tiles = _ffn_w1_tiles(w_ff1)
    w2_bf = bf(w_ff2)
    x, hm = _ffn(hm, w1_tiles, row(b_ff1[0]), w2_bf, row(b_ff2[0]),
                 x, mod, 0, row(ln2_g[0]), row(ln2_b[0]), last=False)

    cos_t, sin_t = _rope_tables()
    proj = _ret_in_proj(hm, w_ret_in[0], cos_t, sin_t)
    rate = jnp.broadcast_to(ret_log2_rate[0][:, :, None, None], (2, HEADS, 1, DV))
    gp, new_state = _retention(rate, proj, None, seq=SEQ, n_seq=BATCH, sps=4,
                               row_block0=0, write_state=True)
    (gs,) = _retention(rate, proj, state_ret, seq=DEC_SEQ, n_seq=DEC_BATCH, sps=1,
                       row_block0=P_TOK // DEC_SEQ, write_state=False)
    x, hm = _proj_res_ln([gp, gs], bf(w_ret_o[0]), jnp.zeros((1, D), F32), [x], mod, 1,
                         row(ln1_g[1]), row(ln1_b[1]), tm=256, sub=128)
    yp, ys = _ffn(hm, w1_tiles, row(b_ff1[1]), w2_bf, row(b_ff2[1]),
                  x, mod, 1, row(ln2_g[1]), row(ln2_b[1]), last=True)

    return (yp.reshape(BATCH, SEQ, D), ys.reshape(DEC_BATCH, DEC_SEQ, D), new_state)
```

```python
import functools

import jax
import jax.numpy as jnp
from jax import lax
from jax.experimental import pallas as pl
from jax.experimental.pallas import tpu as pltpu

F32 = jnp.float32
BF16 = jnp.bfloat16

D = 2048
BATCH = 16
SEQ = 256
DEC_BATCH = 4
DEC_SEQ = 2048
GRID_W = 64
CONV_WIDTH = 31
D_FF = 4 * D
HEADS = 8
DK = D // HEADS
DV = 2 * D // HEADS
ROPE_BASE = 10000.0
LN_EPS = 1e-5
GN_EPS = 1e-5
DEPTH = 2
ALPHA = (2.0 * DEPTH) ** 0.25

P_TOK = BATCH * SEQ
S_TOK = DEC_BATCH * DEC_SEQ
N_TOK = P_TOK + S_TOK
N_COND = 8

LANES = 128
SUBLANES = 8
TM = 512
VMEM_LIMIT = 60 * 1024 * 1024


def _cparams(sem):
    return pltpu.CompilerParams(dimension_semantics=sem, vmem_limit_bytes=VMEM_LIMIT)


def _cond_index(i, tm):
    r = i * tm
    return jnp.where(r < P_TOK, 0, 1 + (r - P_TOK) // DEC_SEQ)


def _mod_spec(layer, part, tm, row_axis=0):
    base = (layer * 6 + part) * N_COND
    return pl.BlockSpec((None, 1, D),
                        lambda *ids: (base + _cond_index(ids[row_axis], tm), 0, 0))


def _const_spec(shape):
    return pl.BlockSpec(shape, lambda *ids: (0,) * len(shape))


def _group_specs(tm, cols, row_axis=0):
    n_p = P_TOK // tm
    ctx = pl.BlockSpec((tm, cols), lambda *ids: (jnp.minimum(ids[row_axis], n_p - 1), 0))
    lat = pl.BlockSpec((tm, cols), lambda *ids: (jnp.maximum(ids[row_axis] - n_p, 0), 0))
    return [ctx, lat]


def _layer_norm(r, g, b):
    mu = jnp.mean(r, axis=-1, keepdims=True)
    d = r - mu
    var = jnp.mean(d * d, axis=-1, keepdims=True)
    return d * lax.rsqrt(var + LN_EPS) * g + b


def _silu(x):
    return x * jax.nn.sigmoid(x)


def _dot(a, b):
    return jnp.dot(a, b, preferred_element_type=F32)


def _mod_kernel(cond_ref, w_ref, b_ref, o_ref):
    s = _silu(cond_ref[...]).astype(BF16)
    o_ref[...] = _dot(s, w_ref[...].astype(BF16)) + b_ref[...]


def _modulation(cond, w_mod, b_mod):
    tn = 1024
    return pl.pallas_call(
        _mod_kernel,
        grid=(DEPTH, 6 * D // tn),
        in_specs=[
            pl.BlockSpec((N_COND, D), lambda l, j: (0, 0)),
            pl.BlockSpec((None, D, tn), lambda l, j: (l, 0, j)),
            pl.BlockSpec((None, 1, tn), lambda l, j: (l, 0, j)),
        ],
        out_specs=pl.BlockSpec((None, N_COND, tn), lambda l, j: (l, 0, j)),
        out_shape=jax.ShapeDtypeStruct((DEPTH, N_COND, 6 * D), F32),
        compiler_params=_cparams(("parallel", "parallel")),
        name="modulation",
    )(cond, w_mod, b_mod.reshape(DEPTH, 1, 6 * D))


CONV_TB = 256
CONV_HALO = 16
CONV_WIN = CONV_TB + 2 * CONV_HALO
CONV_CW = 256
CONV_RB = 32
CONV_SUB = 128


def _dw_taps(win_ref, sh_ref, wb_ref, y_ref, c0):
    shift = CONV_HALO - CONV_WIDTH // 2
    cols = slice(c0, c0 + CONV_CW)
    w0 = win_ref[:, cols]
    for s in range(1, SUBLANES):
        sh_ref[s - 1] = pltpu.roll(w0, CONV_WIN - s, axis=0)
    groups = CONV_RB // SUBLANES
    for r in range(0, CONV_TB, CONV_RB):
        accs = [jnp.zeros((SUBLANES, CONV_CW), F32)] * groups
        for k in range(CONV_WIDTH):
            off = shift + k
            s, base = off % SUBLANES, r + off - off % SUBLANES
            if s == 0:
                src = win_ref[base:base + CONV_RB, cols]
            else:
                src = sh_ref[s - 1, base:base + CONV_RB, :]
            wk = wb_ref[k, :, cols]
            accs = [a + src[g * SUBLANES:(g + 1) * SUBLANES] * wk for g, a in enumerate(accs)]
        y_ref[r:r + CONV_RB, cols] = jnp.concatenate(accs, axis=0)


def _conv_mixer_kernel(xp_ref, xs_ref, prev_ref, next_ref, sc1_ref, sh1_ref, gate_ref,
                       sc2_ref, sh2_ref, w1_ref, b1_ref, wdw_ref, bdw_ref, cng_ref, cnb_ref,
                       w2_ref, b2_ref, lng_ref, lnb_ref, xo_ref, hmo_ref,
                       hm_ref, win_ref, sh_ref, wb_ref, y_ref):
    i = pl.program_id(0)

    @pl.when(i == 0)
    def _():
        for k in range(CONV_WIDTH):
            wb_ref[k] = jnp.broadcast_to(wdw_ref[pl.ds(k, 1), :], (SUBLANES, D))

    n_p = P_TOK // CONV_TB
    per_seq = DEC_SEQ // CONV_TB
    pos = (i - n_p) % per_seq
    latent = i >= n_p
    has_prev = jnp.logical_and(latent, pos != 0)
    has_next = jnp.logical_and(latent, pos != per_seq - 1)
    lo, hi = CONV_HALO, CONV_HALO + CONV_TB

    def modulate(x):
        return (x * (1.0 + sc1_ref[...]) + sh1_ref[...]).astype(BF16)

    hm_ref[0:lo, :] = modulate(prev_ref[...])
    hm_ref[lo:hi, :] = modulate(jnp.where(latent, xs_ref[...], xp_ref[...]))
    hm_ref[hi:, :] = modulate(next_ref[...])

    for c0 in range(0, D, CONV_CW):
        hm = hm_ref[...]
        a = _dot(hm, w1_ref[:, c0:c0 + CONV_CW]) + b1_ref[:, c0:c0 + CONV_CW]
        g = _dot(hm, w1_ref[:, D + c0:D + c0 + CONV_CW]) + b1_ref[:, D + c0:D + c0 + CONV_CW]
        u = a * jax.nn.sigmoid(g)
        win_ref[0:lo, c0:c0 + CONV_CW] = jnp.where(has_prev, u[0:lo], 0.0)
        win_ref[lo:hi, c0:c0 + CONV_CW] = u[lo:hi]
        win_ref[hi:, c0:c0 + CONV_CW] = jnp.where(has_next, u[hi:], 0.0)
        _dw_taps(win_ref, sh_ref, wb_ref, y_ref, c0)

    v = _silu(_layer_norm(y_ref[...] + bdw_ref[...], cng_ref[...], cnb_ref[...])).astype(BF16)
    for s in range(CONV_TB // CONV_SUB):
        rows = slice(s * CONV_SUB, (s + 1) * CONV_SUB)
        y2 = _dot(v[rows], w2_ref[...]) + b2_ref[...]
        x = jnp.where(latent, xs_ref[rows, :], xp_ref[rows, :])
        xn = _layer_norm(ALPHA * x + gate_ref[...] * y2, lng_ref[...], lnb_ref[...])
        xo_ref[rows, :] = xn
        hmo_ref[rows, :] = (xn * (1.0 + sc2_ref[...]) + sh2_ref[...]).astype(BF16)


def _conv_mixer(xp, xs, mod, w_pw1, b_pw1, w_dw, b_dw, cn_g, cn_b, w_pw2, b_pw2, ln_g, ln_b):
    tb = CONV_TB
    n_p = P_TOK // tb
    hb = tb // CONV_HALO
    last_halo = S_TOK // CONV_HALO - 1
    vec = _const_spec((1, D))
    resident = lambda shape: pl.BlockSpec(shape, lambda i: (0, 0), pipeline_mode=pl.Buffered(1))
    return pl.pallas_call(
        _conv_mixer_kernel,
        grid=(N_TOK // tb,),
        in_specs=_group_specs(tb, D) + [
            pl.BlockSpec((CONV_HALO, D), lambda i: (jnp.clip((i - n_p) * hb - 1, 0, last_halo), 0)),
            pl.BlockSpec((CONV_HALO, D), lambda i: (jnp.clip((i - n_p + 1) * hb, 0, last_halo), 0)),
            _mod_spec(0, 1, tb), _mod_spec(0, 0, tb), _mod_spec(0, 2, tb),
            _mod_spec(0, 4, tb), _mod_spec(0, 3, tb),
            resident((D, 2 * D)), _const_spec((1, 2 * D)),
            _const_spec((CONV_WIDTH, D)), vec, vec, vec,
            resident((D, D)), vec, vec, vec,
        ],
        out_specs=[pl.BlockSpec((tb, D), lambda i: (i, 0)),
                   pl.BlockSpec((tb, D), lambda i: (i, 0))],
        out_shape=[jax.ShapeDtypeStruct((N_TOK, D), F32),
                   jax.ShapeDtypeStruct((N_TOK, D), BF16)],
        scratch_shapes=[pltpu.VMEM((CONV_WIN, D), BF16),
                        pltpu.VMEM((CONV_WIN, D), F32),
                        pltpu.VMEM((SUBLANES - 1, CONV_WIN, CONV_CW), F32),
                        pltpu.VMEM((CONV_WIDTH, SUBLANES, D), F32),
                        pltpu.VMEM((CONV_TB, D), F32)],
        compiler_params=_cparams(("arbitrary",)),
        name="conv_mixer",
    )(xp, xs, xs, xs, mod, mod, mod, mod, mod, w_pw1, b_pw1, w_dw, b_dw, cn_g, cn_b,
      w_pw2, b_pw2, ln_g, ln_b)


EPI_SUB = 256


def _proj_res_ln_kernel(*refs, n_p, lhs_split, res_split, tm, sub):
    refs = list(refs)
    a_refs = [refs.pop(0) for _ in range(2 if lhs_split else 1)]
    w_ref, b_ref = refs.pop(0), refs.pop(0)
    x_refs = [refs.pop(0) for _ in range(2 if res_split else 1)]
    gate_ref, lng_ref, lnb_ref, sc_ref, sh_ref, xo_ref, hmo_ref = refs
    i = pl.program_id(0)

    def pick(pair, rows):
        if len(pair) == 2:
            return jnp.where(i < n_p, pair[0][rows, :], pair[1][rows, :])
        return pair[0][rows, :]

    for s in range(tm // sub):
        rows = pl.ds(s * sub, sub)
        y = _dot(pick(a_refs, rows), w_ref[...]) + b_ref[...]
        xn = _layer_norm(ALPHA * pick(x_refs, rows) + gate_ref[...] * y,
                         lng_ref[...], lnb_ref[...])
        xo_ref[rows, :] = xn
        hmo_ref[rows, :] = (xn * (1.0 + sc_ref[...]) + sh_ref[...]).astype(BF16)


def _proj_res_ln(a_list, w, b, x_list, mod, layer, ln_g, ln_b, tm, sub):
    kdim = w.shape[0]
    n_p = P_TOK // tm
    lhs_split, res_split = len(a_list) == 2, len(x_list) == 2

    def row_specs(split, cols):
        if not split:
            return [pl.BlockSpec((tm, cols), lambda i: (i, 0))]
        return _group_specs(tm, cols)

    in_specs = (row_specs(lhs_split, kdim)
                + [pl.BlockSpec((kdim, D), lambda i: (0, 0), pipeline_mode=pl.Buffered(1)),
                   _const_spec((1, D))]
                + row_specs(res_split, D)
                + [_mod_spec(layer, 2, tm), _const_spec((1, D)), _const_spec((1, D)),
                   _mod_spec(layer, 4, tm), _mod_spec(layer, 3, tm)])
    return pl.pallas_call(
        functools.partial(_proj_res_ln_kernel, n_p=n_p, lhs_split=lhs_split,
                          res_split=res_split, tm=tm, sub=sub),
        grid=(N_TOK // tm,),
        in_specs=in_specs,
        out_specs=[pl.BlockSpec((tm, D), lambda i: (i, 0)),
                   pl.BlockSpec((tm, D), lambda i: (i, 0))],
        out_shape=[jax.ShapeDtypeStruct((N_TOK, D), F32),
                   jax.ShapeDtypeStruct((N_TOK, D), BF16)],
        compiler_params=_cparams(("parallel",)),
        name=f"proj_res_ln_l{layer}",
    )(*a_list, w, b, *x_list, mod, ln_g, ln_b, mod, mod)


def _ffn_kernel(*refs, nf, n_p, last):
    hm_ref, w1_ref, b1_ref, w2_ref, b2_ref, x_ref, gate_ref, lng_ref, lnb_ref = refs[:9]
    if last:
        yp_ref, ys_ref, acc_ref = refs[9:]
    else:
        sc_ref, sh_ref, xo_ref, hmo_ref, acc_ref = refs[9:]
    i = pl.program_id(0)
    f = pl.program_id(1)
    tm = hm_ref.shape[0]

    h = _dot(hm_ref[...], w1_ref[...]) + b1_ref[...]
    h = jnp.square(jnp.maximum(h, 0.0)).astype(BF16)

    @pl.when(f == 0)
    def _():
        acc_ref[...] = _dot(h, w2_ref[...])

    @pl.when(jnp.logical_and(f > 0, f < nf - 1))
    def _():
        acc_ref[...] += _dot(h, w2_ref[...])

    def finish(store):
        for s in range(tm // EPI_SUB):
            rows = pl.ds(s * EPI_SUB, EPI_SUB)
            y = (acc_ref[rows, :] + _dot(h[s * EPI_SUB:(s + 1) * EPI_SUB], w2_ref[...])
                 + b2_ref[...])
            store(rows, _layer_norm(ALPHA * x_ref[rows, :] + gate_ref[...] * y,
                                    lng_ref[...], lnb_ref[...]))

    def store_ctx(rows, xn):
        yp_ref[rows, :] = xn

    def store_lat(rows, xn):
        ys_ref[rows, :] = xn

    def store_next(rows, xn):
        xo_ref[rows, :] = xn
        hmo_ref[rows, :] = (xn * (1.0 + sc_ref[...]) + sh_ref[...]).astype(BF16)

    is_last_f = f == nf - 1
    if last:
        pl.when(jnp.logical_and(is_last_f, i < n_p))(lambda: finish(store_ctx))
        pl.when(jnp.logical_and(is_last_f, i >= n_p))(lambda: finish(store_lat))
    else:
        pl.when(is_last_f)(lambda: finish(store_next))


FFN_TF = 1024


def _w1_tiles_kernel(w_ref, o_ref):
    o_ref[...] = w_ref[...].astype(BF16)


def _ffn_w1_tiles(w_ff1):
    nf = D_FF // FFN_TF
    return pl.pallas_call(
        _w1_tiles_kernel,
        grid=(DEPTH, nf),
        in_specs=[pl.BlockSpec((None, D, FFN_TF), lambda l, f: (l, 0, f))],
        out_specs=pl.BlockSpec((None, None, D, FFN_TF), lambda l, f: (l, f, 0, 0)),
        out_shape=jax.ShapeDtypeStruct((DEPTH, nf, D, FFN_TF), BF16),
        compiler_params=_cparams(("parallel", "parallel")),
        name="ffn_w1_tiles",
    )(w_ff1)


def _ffn(hm, w1_tiles, b1, w2, b2, x, mod, layer, ln_g, ln_b, last):
    tm, tf = TM, FFN_TF
    nf = D_FF // tf
    n_p = P_TOK // tm
    in_specs = [
        pl.BlockSpec((tm, D), lambda i, f: (i, 0)),
        pl.BlockSpec((None, None, D, tf), lambda i, f: (layer, f, 0, 0)),
        pl.BlockSpec((1, tf), lambda i, f: (0, f)),
        pl.BlockSpec((None, tf, D), lambda i, f: (layer, f, 0)),
        _const_spec((1, D)),
        pl.BlockSpec((tm, D), lambda i, f: (i, 0)),
        _mod_spec(layer, 5, tm),
        _const_spec((1, D)), _const_spec((1, D)),
    ]
    args = [hm, w1_tiles, b1, w2, b2, x, mod, ln_g, ln_b]
    if last:
        out_specs = _group_specs(tm, D)
        out_shape = [jax.ShapeDtypeStruct((P_TOK, D), F32), jax.ShapeDtypeStruct((S_TOK, D), F32)]
    else:
        in_specs += [_mod_spec(layer + 1, 1, tm), _mod_spec(layer + 1, 0, tm)]
        args += [mod, mod]
        out_specs = [pl.BlockSpec((tm, D), lambda i, f: (i, 0))] * 2
        out_shape = [jax.ShapeDtypeStruct((N_TOK, D), F32), jax.ShapeDtypeStruct((N_TOK, D), BF16)]
    return pl.pallas_call(
        functools.partial(_ffn_kernel, nf=nf, n_p=n_p, last=last),
        grid=(N_TOK // tm, nf),
        in_specs=in_specs,
        out_specs=out_specs,
        out_shape=out_shape,
        scratch_shapes=[pltpu.VMEM((tm, D), F32)],
        compiler_params=_cparams(("arbitrary" if last else "parallel", "arbitrary")),
        name=f"ffn_l{layer}",
    )(*args)


def _rope_tile(y, cos_ref, sin_ref):
    half = DK // 2
    outs = []
    for s in range(y.shape[1] // half):
        x = y[:, s * half:(s + 1) * half]
        t = (s % 2) * half
        rot = pltpu.roll(x, half // 2, axis=1)
        outs.append(x * cos_ref[:, t:t + half] + rot * sin_ref[:, t:t + half])
    return jnp.concatenate(outs, axis=1)


def _ret_in_kernel(hm_ref, w_ref, cos_ref, sin_ref, o_ref, w_bf, *, tm, tn):
    j = pl.program_id(0)
    i = pl.program_id(1)

    @pl.when(i == 0)
    def _():
        w_bf[...] = w_ref[...].astype(BF16)

    y = _dot(hm_ref[...], w_bf[...])
    y = y * jnp.where(j < D // tn, DK ** -0.5, 1.0)
    use_rope = jnp.logical_and(j < 2 * D // tn, i >= P_TOK // tm)

    def store(t):
        for s in range(tn // DK):
            o_ref[s] = t[:, s * DK:(s + 1) * DK].astype(BF16)

    @pl.when(use_rope)
    def _():
        store(_rope_tile(y, cos_ref, sin_ref))

    @pl.when(jnp.logical_not(use_rope))
    def _():
        store(y)


def _ret_in_proj(hm, w_in, cos_t, sin_t):
    tm, tn = 1024, 1024
    n_p = P_TOK // tm
    tpos = lambda j, i: (jnp.maximum(i - n_p, 0) % (DEC_SEQ // tm), 0)
    return pl.pallas_call(
        functools.partial(_ret_in_kernel, tm=tm, tn=tn),
        grid=(6 * D // tn, N_TOK // tm),
        in_specs=[
            pl.BlockSpec((tm, D), lambda j, i: (i, 0)),
            pl.BlockSpec((D, tn), lambda j, i: (0, j)),
            pl.BlockSpec((tm, DK), tpos),
            pl.BlockSpec((tm, DK), tpos),
        ],
        out_specs=pl.BlockSpec((tn // DK, tm, DK), lambda j, i: (j, i, 0)),
        out_shape=jax.ShapeDtypeStruct((6 * D // DK, N_TOK, DK), BF16),
        scratch_shapes=[pltpu.VMEM((D, tn), BF16)],
        compiler_params=_cparams(("parallel", "arbitrary")),
        name="ret_in_proj",
    )(hm, w_in, cos_t, sin_t)


RC = 256


def _scale_rows(a, d):
    return jnp.concatenate(
        [a[:, s * LANES:(s + 1) * LANES] * d for s in range(a.shape[1] // LANES)], axis=1)


def _two_slabs(ref, rows):
    return jnp.concatenate([ref[0, rows, :], ref[1, rows, :]], axis=1)


def _retention_kernel(*refs, seq, sps, has_s0, write_state):
    refs = list(refs)
    rate_ref, q_ref, k_ref, v_ref, g_ref = [refs.pop(0) for _ in range(5)]
    s0_ref = refs.pop(0) if has_s0 else None
    o_ref = refs.pop(0)
    so_ref = refs.pop(0) if write_state else None
    mask_ref, din_ref, dout_ref, sb_all, s_all = refs
    n = seq // RC
    c_f = float(RC)

    lg = jnp.log1p(-jnp.exp2(rate_ref[...]))
    lgf, lgb = lg[0], lg[1]
    dchunk_f = jnp.exp(lgf * c_f)
    dchunk_b = jnp.exp(lgb * c_f)

    @pl.when(pl.program_id(1) == 0)
    def _():
        rows = lax.broadcasted_iota(jnp.int32, (RC, LANES), 0).astype(F32)
        diff = (lax.broadcasted_iota(jnp.int32, (RC, RC), 0)
                - lax.broadcasted_iota(jnp.int32, (RC, RC), 1)).astype(F32)
        mf = jnp.where(diff >= 0, jnp.exp(lgf[:, :RC] * jnp.maximum(diff, 0.0)), 0.0)
        mb = jnp.where(diff <= 0, jnp.exp(lgb[:, :RC] * jnp.maximum(-diff, 0.0)), 0.0)
        mask_ref[...] = mf + mb
        lf, lb = lgf[:, :LANES], lgb[:, :LANES]
        din_ref[0] = jnp.exp(lf * (rows + 1.0))
        din_ref[1] = jnp.exp(lb * (c_f - rows))
        dout_ref[0] = jnp.exp(lf * (c_f - 1.0 - rows))
        dout_ref[1] = jnp.exp(lb * rows)

    def loop(body, unroll):
        if n <= 2:
            for t in range(n):
                body(t)
        else:
            lax.fori_loop(0, n, lambda t, carry: (body(t), carry)[1], 0, unroll=unroll)

    for sq in range(sps):
        sb_ref, s_ref = sb_all.at[sq], s_all.at[sq]

        def chunk_rows(c, sq=sq):
            if isinstance(c, int):
                return pl.ds(sq * seq + c * RC, RC)
            return pl.ds(pl.multiple_of(sq * seq + c * RC, RC), RC)

        def init_state(direction, sq=sq, s_ref=s_ref):
            if has_s0:
                s_ref[...] = s0_ref[sq, direction]
            else:
                s_ref[...] = jnp.zeros((DK, DV), F32)

        def state_update(c, direction, dchunk, chunk_rows=chunk_rows, s_ref=s_ref):
            rows = chunk_rows(c)
            kd = _scale_rows(k_ref[rows, :].astype(F32), dout_ref[direction]).astype(BF16)
            upd = lax.dot_general(kd, _two_slabs(v_ref, rows), (((0,), (0,)), ((), ())),
                                  preferred_element_type=F32)
            s_ref[...] = dchunk * s_ref[...] + upd

        init_state(1)

        def pass1(t, sb_ref=sb_ref, s_ref=s_ref, state_update=state_update):
            c = n - 1 - t
            sb_ref[c] = s_ref[...].astype(BF16)
            state_update(c, 1, dchunk_b)

        loop(pass1, 1)
        if write_state:
            so_ref[sq, 1] = s_ref[...]

        init_state(0)

        def pass2(c, sb_ref=sb_ref, s_ref=s_ref, chunk_rows=chunk_rows,
                  state_update=state_update):
            rows = chunk_rows(c)
            qc = q_ref[rows, :]
            scores = lax.dot_general(qc, k_ref[rows, :], (((1,), (1,)), ((), ())),
                                     preferred_element_type=F32)
            tot = _dot((scores * mask_ref[...]).astype(BF16), _two_slabs(v_ref, rows))
            if has_s0 or not (isinstance(c, int) and c == 0):
                tot = tot + _scale_rows(_dot(qc, s_ref[...].astype(BF16)), din_ref[0])
            if has_s0 or not (isinstance(c, int) and c == n - 1):
                tot = tot + _scale_rows(_dot(qc, sb_ref[c]), din_ref[1])
            mu = jnp.mean(tot, axis=-1, keepdims=True)
            d = tot - mu
            var = jnp.mean(d * d, axis=-1, keepdims=True)
            nrm = d * lax.rsqrt(var + GN_EPS)
            gate = _silu(_two_slabs(g_ref, rows).astype(F32))
            o_ref[rows, :] = (gate * nrm).astype(BF16)
            state_update(c, 0, dchunk_f)

        loop(pass2, 2)
        if write_state:
            so_ref[sq, 0] = s_ref[...]


def _retention(rate, proj, s0, *, seq, n_seq, sps, row_block0, write_state):
    has_s0 = s0 is not None
    koff, voff, goff = D // DK, 2 * D // DV, 4 * D // DV
    rows = sps * seq
    state_block = (sps, None, 2, None, DK, DV)
    in_specs = [
        pl.BlockSpec((2, None, 1, DV), lambda h, b: (0, h, 0, 0)),
        pl.BlockSpec((None, rows, DK), lambda h, b: (h, row_block0 + b, 0)),
        pl.BlockSpec((None, rows, DK), lambda h, b: (koff + h, row_block0 + b, 0)),
        pl.BlockSpec((2, rows, DK), lambda h, b: (voff + h, row_block0 + b, 0)),
        pl.BlockSpec((2, rows, DK), lambda h, b: (goff + h, row_block0 + b, 0)),
    ]
    args = [rate, proj, proj, proj, proj]
    if has_s0:
        in_specs.append(pl.BlockSpec(state_block, lambda h, b: (b, 0, 0, h, 0, 0)))
        args.append(s0)
    out_specs = [pl.BlockSpec((rows, DV), lambda h, b: (b, h))]
    out_shape = [jax.ShapeDtypeStruct((n_seq * seq, 2 * D), BF16)]
    if write_state:
        out_specs.append(pl.BlockSpec(state_block, lambda h, b: (b, 0, 0, h, 0, 0)))
        out_shape.append(jax.ShapeDtypeStruct((n_seq, 1, 2, HEADS, DK, DV), F32))
    return pl.pallas_call(
        functools.partial(_retention_kernel, seq=seq, sps=sps, has_s0=has_s0,
                          write_state=write_state),
        grid=(HEADS, n_seq // sps),
        in_specs=in_specs,
        out_specs=out_specs,
        out_shape=out_shape,
        scratch_shapes=[pltpu.VMEM((RC, RC), F32),
                        pltpu.VMEM((2, RC, LANES), F32),
                        pltpu.VMEM((2, RC, LANES), F32),
                        pltpu.VMEM((sps, seq // RC, DK, DV), BF16),
                        pltpu.VMEM((sps, DK, DV), F32)],
        compiler_params=_cparams(("parallel", "arbitrary")),
        name=f"retention_t{seq}",
    )(*args)


def _rope_tables():
    t = jnp.arange(DEC_SEQ)
    row = (t // GRID_W).astype(F32)
    col = (t % GRID_W).astype(F32)
    quarter = DK // 4
    inv = ROPE_BASE ** (-jnp.arange(quarter, dtype=F32) / quarter)
    ang_r = row[:, None] * inv
    ang_c = col[:, None] * inv
    cos_t = jnp.concatenate([jnp.cos(ang_r)] * 2 + [jnp.cos(ang_c)] * 2, axis=1)
    sin_t = jnp.concatenate([-jnp.sin(ang_r), jnp.sin(ang_r), -jnp.sin(ang_c), jnp.sin(ang_c)], axis=1)
    return cos_t, sin_t


def kernel(x_prompt, x_sample, state_ret, c, c_ctx, w_mod, b_mod, ln1_g, ln1_b, ln2_g, ln2_b,
           w_pw1, b_pw1, w_dw, b_dw, cn_g, cn_b, w_pw2, b_pw2,
           w_ret_in, ret_log2_rate, w_ret_o, w_ff1, b_ff1, w_ff2, b_ff2):
    xp = x_prompt.reshape(P_TOK, D)
    xs = x_sample.reshape(S_TOK, D)
    cond = jnp.concatenate(
        [c_ctx[None, :], c, jnp.zeros((N_COND - 1 - DEC_BATCH, D), F32)], axis=0)

    mod = _modulation(cond, w_mod, b_mod)
    mod = mod.reshape(DEPTH, N_COND, 6, D).transpose(0, 2, 1, 3).reshape(DEPTH * 6 * N_COND, 1, D)

    bf = lambda w: w.astype(BF16)
    row = lambda v: v.reshape(1, -1)

    x, hm = _conv_mixer(xp, xs, mod, bf(w_pw1[0]), row(b_pw1[0]), w_dw[0], row(b_dw[0]),
                        row(cn_g[0]), row(cn_b[0]), bf(w_pw2[0]), row(b_pw2[0]),
                        row(ln1_g[0]), row(ln1_b[0]))
    w1_tiles = _ffn_w1_tiles(w_ff1)
    w2_bf = bf(w_ff2)
    x, hm = _ffn(hm, w1_tiles, row(b_ff1[0]), w2_bf, row(b_ff2[0]),
                 x, mod, 0, row(ln2_g[0]), row(ln2_b[0]), last=False)

    cos_t, sin_t = _rope_tables()
    proj = _ret_in_proj(hm, w_ret_in[0], cos_t, sin_t)
    rate = jnp.broadcast_to(ret_log2_rate[0][:, :, None, None], (2, HEADS, 1, DV))
    gp, new_state = _retention(rate, proj, None, seq=SEQ, n_seq=BATCH, sps=4,
                               row_block0=0, write_state=True)
    (gs,) = _retention(rate, proj, state_ret, seq=DEC_SEQ, n_seq=DEC_BATCH, sps=1,
                       row_block0=P_TOK // DEC_SEQ, write_state=False)
    x, hm = _proj_res_ln([gp, gs], bf(w_ret_o[0]), jnp.zeros((1, D), F32), [x], mod, 1,
                         row(ln1_g[1]), row(ln1_b[1]), tm=256, sub=128)
    yp, ys = _ffn(hm, w1_tiles, row(b_ff1[1]), w2_bf, row(b_ff2[1]),
                  x, mod, 1, row(ln2_g[1]), row(ln2_b[1]), last=True)

    return (yp.reshape(BATCH, SEQ, D), ys.reshape(DEC_BATCH, DEC_SEQ, D), new_state)
```

```python
import functools

import jax
import jax.numpy as jnp
from jax import lax
from jax.experimental import pallas as pl
from jax.experimental.pallas import tpu as pltpu

F32 = jnp.float32
BF16 = jnp.bfloat16

D = 2048
BATCH = 16
SEQ = 256
DEC_BATCH = 4
DEC_SEQ = 2048
GRID_W = 64
CONV_WIDTH = 31
D_FF = 4 * D
HEADS = 8
DK = D // HEADS
DV = 2 * D // HEADS
ROPE_BASE = 10000.0
LN_EPS = 1e-5
GN_EPS = 1e-5
DEPTH = 2
ALPHA = (2.0 * DEPTH) ** 0.25

P_TOK = BATCH * SEQ
S_TOK = DEC_BATCH * DEC_SEQ
N_TOK = P_TOK + S_TOK
N_COND = 8

LANES = 128
SUBLANES = 8
TM = 512
VMEM_LIMIT = 60 * 1024 * 1024


def _cparams(sem):
    return pltpu.CompilerParams(dimension_semantics=sem, vmem_limit_bytes=VMEM_LIMIT)


def _cond_index(i, tm):
    r = i * tm
    return jnp.where(r < P_TOK, 0, 1 + (r - P_TOK) // DEC_SEQ)


def _mod_spec(layer, part, tm, row_axis=0):
    base = (layer * 6 + part) * N_COND
    return pl.BlockSpec((None, 1, D),
                        lambda *ids: (base + _cond_index(ids[row_axis], tm), 0, 0))


def _const_spec(shape):
    return pl.BlockSpec(shape, lambda *ids: (0,) * len(shape))


def _group_specs(tm, cols, row_axis=0):
    n_p = P_TOK // tm
    ctx = pl.BlockSpec((tm, cols), lambda *ids: (jnp.minimum(ids[row_axis], n_p - 1), 0))
    lat = pl.BlockSpec((tm, cols), lambda *ids: (jnp.maximum(ids[row_axis] - n_p, 0), 0))
    return [ctx, lat]


def _layer_norm(r, g, b):
    mu = jnp.mean(r, axis=-1, keepdims=True)
    d = r - mu
    var = jnp.mean(d * d, axis=-1, keepdims=True)
    return d * lax.rsqrt(var + LN_EPS) * g + b


def _silu(x):
    return x * jax.nn.sigmoid(x)


def _dot(a, b):
    return jnp.dot(a, b, preferred_element_type=F32)


def _mod_kernel(cond_ref, w_ref, b_ref, o_ref):
    s = _silu(cond_ref[...]).astype(BF16)
    o_ref[...] = _dot(s, w_ref[...].astype(BF16)) + b_ref[...]


def _modulation(cond, w_mod, b_mod):
    tn = 1024
    return pl.pallas_call(
        _mod_kernel,
        grid=(DEPTH, 6 * D // tn),
        in_specs=[
            pl.BlockSpec((N_COND, D), lambda l, j: (0, 0)),
            pl.BlockSpec((None, D, tn), lambda l, j: (l, 0, j)),
            pl.BlockSpec((None, 1, tn), lambda l, j: (l, 0, j)),
        ],
        out_specs=pl.BlockSpec((None, N_COND, tn), lambda l, j: (l, 0, j)),
        out_shape=jax.ShapeDtypeStruct((DEPTH, N_COND, 6 * D), F32),
        compiler_params=_cparams(("parallel", "parallel")),
        name="modulation",
    )(cond, w_mod, b_mod.reshape(DEPTH, 1, 6 * D))


CONV_TB = 256
CONV_HALO = 16
CONV_WIN = CONV_TB + 2 * CONV_HALO
CONV_CW = 256
CONV_RB = 32
CONV_SUB = 128


def _dw_taps(win_ref, sh_ref, wb_ref, y_ref, c0):
    shift = CONV_HALO - CONV_WIDTH // 2
    cols = slice(c0, c0 + CONV_CW)
    w0 = win_ref[:, cols]
    for s in range(1, SUBLANES):
        sh_ref[s - 1] = pltpu.roll(w0, CONV_WIN - s, axis=0)
    groups = CONV_RB // SUBLANES
    for r in range(0, CONV_TB, CONV_RB):
        accs = [jnp.zeros((SUBLANES, CONV_CW), F32)] * groups
        for k in range(CONV_WIDTH):
            off = shift + k
            s, base = off % SUBLANES, r + off - off % SUBLANES
            if s == 0:
                src = win_ref[base:base + CONV_RB, cols]
            else:
                src = sh_ref[s - 1, base:base + CONV_RB, :]
            wk = wb_ref[k, :, cols]
            accs = [a + src[g * SUBLANES:(g + 1) * SUBLANES] * wk for g, a in enumerate(accs)]
        y_ref[r:r + CONV_RB, cols] = jnp.concatenate(accs, axis=0)


def _conv_mixer_kernel(xp_ref, xs_ref, prev_ref, next_ref, sc1_ref, sh1_ref, gate_ref,
                       sc2_ref, sh2_ref, w1_ref, b1_ref, wdw_ref, bdw_ref, cng_ref, cnb_ref,
                       w2_ref, b2_ref, lng_ref, lnb_ref, xo_ref, hmo_ref,
                       hm_ref, win_ref, sh_ref, wb_ref, y_ref):
    i = pl.program_id(0)

    @pl.when(i == 0)
    def _():
        for k in range(CONV_WIDTH):
            wb_ref[k] = jnp.broadcast_to(wdw_ref[pl.ds(k, 1), :], (SUBLANES, D))

    n_p = P_TOK // CONV_TB
    per_seq = DEC_SEQ // CONV_TB
    pos = (i - n_p) % per_seq
    latent = i >= n_p
    has_prev = jnp.logical_and(latent, pos != 0)
    has_next = jnp.logical_and(latent, pos != per_seq - 1)
    lo, hi = CONV_HALO, CONV_HALO + CONV_TB

    def modulate(x):
        return (x * (1.0 + sc1_ref[...]) + sh1_ref[...]).astype(BF16)

    hm_ref[0:lo, :] = modulate(prev_ref[...])
    hm_ref[lo:hi, :] = modulate(jnp.where(latent, xs_ref[...], xp_ref[...]))
    hm_ref[hi:, :] = modulate(next_ref[...])

    for c0 in range(0, D, CONV_CW):
        hm = hm_ref[...]
        a = _dot(hm, w1_ref[:, c0:c0 + CONV_CW]) + b1_ref[:, c0:c0 + CONV_CW]
        g = _dot(hm, w1_ref[:, D + c0:D + c0 + CONV_CW]) + b1_ref[:, D + c0:D + c0 + CONV_CW]
        u = a * jax.nn.sigmoid(g)
        win_ref[0:lo, c0:c0 + CONV_CW] = jnp.where(has_prev, u[0:lo], 0.0)
        win_ref[lo:hi, c0:c0 + CONV_CW] = u[lo:hi]
        win_ref[hi:, c0:c0 + CONV_CW] = jnp.where(has_next, u[hi:], 0.0)
        _dw_taps(win_ref, sh_ref, wb_ref, y_ref, c0)

    v = _silu(_layer_norm(y_ref[...] + bdw_ref[...], cng_ref[...], cnb_ref[...])).astype(BF16)
    for s in range(CONV_TB // CONV_SUB):
        rows = slice(s * CONV_SUB, (s + 1) * CONV_SUB)
        y2 = _dot(v[rows], w2_ref[...]) + b2_ref[...]
        x = jnp.where(latent, xs_ref[rows, :], xp_ref[rows, :])
        xn = _layer_norm(ALPHA * x + gate_ref[...] * y2, lng_ref[...], lnb_ref[...])
        xo_ref[rows, :] = xn
        hmo_ref[rows, :] = (xn * (1.0 + sc2_ref[...]) + sh2_ref[...]).astype(BF16)


def _conv_mixer(xp, xs, mod, w_pw1, b_pw1, w_dw, b_dw, cn_g, cn_b, w_pw2, b_pw2, ln_g, ln_b):
    tb = CONV_TB
    n_p = P_TOK // tb
    hb = tb // CONV_HALO
    last_halo = S_TOK // CONV_HALO - 1
    vec = _const_spec((1, D))
    resident = lambda shape: pl.BlockSpec(shape, lambda i: (0, 0), pipeline_mode=pl.Buffered(1))
    return pl.pallas_call(
        _conv_mixer_kernel,
        grid=(N_TOK // tb,),
        in_specs=_group_specs(tb, D) + [
            pl.BlockSpec((CONV_HALO, D), lambda i: (jnp.clip((i - n_p) * hb - 1, 0, last_halo), 0)),
            pl.BlockSpec((CONV_HALO, D), lambda i: (jnp.clip((i - n_p + 1) * hb, 0, last_halo), 0)),
            _mod_spec(0, 1, tb), _mod_spec(0, 0, tb), _mod_spec(0, 2, tb),
            _mod_spec(0, 4, tb), _mod_spec(0, 3, tb),
            resident((D, 2 * D)), _const_spec((1, 2 * D)),
            _const_spec((CONV_WIDTH, D)), vec, vec, vec,
            resident((D, D)), vec, vec, vec,
        ],
        out_specs=[pl.BlockSpec((tb, D), lambda i: (i, 0)),
                   pl.BlockSpec((tb, D), lambda i: (i, 0))],
        out_shape=[jax.ShapeDtypeStruct((N_TOK, D), F32),
                   jax.ShapeDtypeStruct((N_TOK, D), BF16)],
        scratch_shapes=[pltpu.VMEM((CONV_WIN, D), BF16),
                        pltpu.VMEM((CONV_WIN, D), F32),
                        pltpu.VMEM((SUBLANES - 1, CONV_WIN, CONV_CW), F32),
                        pltpu.VMEM((CONV_WIDTH, SUBLANES, D), F32),
                        pltpu.VMEM((CONV_TB, D), F32)],
        compiler_params=_cparams(("arbitrary",)),
        name="conv_mixer",
    )(xp, xs, xs, xs, mod, mod, mod, mod, mod, w_pw1, b_pw1, w_dw, b_dw, cn_g, cn_b,
      w_pw2, b_pw2, ln_g, ln_b)


EPI_SUB = 256


def _proj_res_ln_kernel(*refs, n_p, lhs_split, res_split, tm, sub):
    refs = list(refs)
    a_refs = [refs.pop(0) for _ in range(2 if lhs_split else 1)]
    w_ref, b_ref = refs.pop(0), refs.pop(0)
    x_refs = [refs.pop(0) for _ in range(2 if res_split else 1)]
    gate_ref, lng_ref, lnb_ref, sc_ref, sh_ref, xo_ref, hmo_ref = refs
    i = pl.program_id(0)

    def pick(pair, rows):
        if len(pair) == 2:
            return jnp.where(i < n_p, pair[0][rows, :], pair[1][rows, :])
        return pair[0][rows, :]

    for s in range(tm // sub):
        rows = pl.ds(s * sub, sub)
        y = _dot(pick(a_refs, rows), w_ref[...]) + b_ref[...]
        xn = _layer_norm(ALPHA * pick(x_refs, rows) + gate_ref[...] * y,
                         lng_ref[...], lnb_ref[...])
        xo_ref[rows, :] = xn
        hmo_ref[rows, :] = (xn * (1.0 + sc_ref[...]) + sh_ref[...]).astype(BF16)


def _proj_res_ln(a_list, w, b, x_list, mod, layer, ln_g, ln_b, tm, sub):
    kdim = w.shape[0]
    n_p = P_TOK // tm
    lhs_split, res_split = len(a_list) == 2, len(x_list) == 2

    def row_specs(split, cols):
        if not split:
            return [pl.BlockSpec((tm, cols), lambda i: (i, 0))]
        return _group_specs(tm, cols)

    in_specs = (row_specs(lhs_split, kdim)
                + [pl.BlockSpec((kdim, D), lambda i: (0, 0), pipeline_mode=pl.Buffered(1)),
                   _const_spec((1, D))]
                + row_specs(res_split, D)
                + [_mod_spec(layer, 2, tm), _const_spec((1, D)), _const_spec((1, D)),
                   _mod_spec(layer, 4, tm), _mod_spec(layer, 3, tm)])
    return pl.pallas_call(
        functools.partial(_proj_res_ln_kernel, n_p=n_p, lhs_split=lhs_split,
                          res_split=res_split, tm=tm, sub=sub),
        grid=(N_TOK // tm,),
        in_specs=in_specs,
        out_specs=[pl.BlockSpec((tm, D), lambda i: (i, 0)),
                   pl.BlockSpec((tm, D), lambda i: (i, 0))],
        out_shape=[jax.ShapeDtypeStruct((N_TOK, D), F32),
                   jax.ShapeDtypeStruct((N_TOK, D), BF16)],
        compiler_params=_cparams(("parallel",)),
        name=f"proj_res_ln_l{layer}",
    )(*a_list, w, b, *x_list, mod, ln_g, ln_b, mod, mod)


def _ffn_kernel(*refs, nf, n_p, last):
    hm_ref, w1_ref, b1_ref, w2_ref, b2_ref, x_ref, gate_ref, lng_ref, lnb_ref = refs[:9]
    if last:
        yp_ref, ys_ref, acc_ref = refs[9:]
    else:
        sc_ref, sh_ref, xo_ref, hmo_ref, acc_ref = refs[9:]
    i = pl.program_id(0)
    f = pl.program_id(1)
    tm = hm_ref.shape[0]

    h = _dot(hm_ref[...], w1_ref[...]) + b1_ref[...]
    h = jnp.square(jnp.maximum(h, 0.0)).astype(BF16)

    @pl.when(f == 0)
    def _():
        acc_ref[...] = _dot(h, w2_ref[...])

    @pl.when(jnp.logical_and(f > 0, f < nf - 1))
    def _():
        acc_ref[...] += _dot(h, w2_ref[...])

    def finish(store):
        for s in range(tm // EPI_SUB):
            rows = pl.ds(s * EPI_SUB, EPI_SUB)
            y = (acc_ref[rows, :] + _dot(h[s * EPI_SUB:(s + 1) * EPI_SUB], w2_ref[...])
                 + b2_ref[...])
            store(rows, _layer_norm(ALPHA * x_ref[rows, :] + gate_ref[...] * y,
                                    lng_ref[...], lnb_ref[...]))

    def store_ctx(rows, xn):
        yp_ref[rows, :] = xn

    def store_lat(rows, xn):
        ys_ref[rows, :] = xn

    def store_next(rows, xn):
        xo_ref[rows, :] = xn
        hmo_ref[rows, :] = (xn * (1.0 + sc_ref[...]) + sh_ref[...]).astype(BF16)

    is_last_f = f == nf - 1
    if last:
        pl.when(jnp.logical_and(is_last_f, i < n_p))(lambda: finish(store_ctx))
        pl.when(jnp.logical_and(is_last_f, i >= n_p))(lambda: finish(store_lat))
    else:
        pl.when(is_last_f)(lambda: finish(store_next))


FFN_TF = 1024


def _w1_tiles_kernel(w_ref, o_ref):
    o_ref[...] = w_ref[...].astype(BF16)


def _ffn_w1_tiles(w_ff1):
    nf = D_FF // FFN_TF
    return pl.pallas_call(
        _w1_tiles_kernel,
        grid=(DEPTH, nf),
        in_specs=[pl.BlockSpec((None, D, FFN_TF), lambda l, f: (l, 0, f))],
        out_specs=pl.BlockSpec((None, None, D, FFN_TF), lambda l, f: (l, f, 0, 0)),
        out_shape=jax.ShapeDtypeStruct((DEPTH, nf, D, FFN_TF), BF16),
        compiler_params=_cparams(("parallel", "parallel")),
        name="ffn_w1_tiles",
    )(w_ff1)


def _ffn(hm, w1_tiles, b1, w2, b2, x, mod, layer, ln_g, ln_b, last):
    tm, tf = TM, FFN_TF
    nf = D_FF // tf
    n_p = P_TOK // tm
    in_specs = [
        pl.BlockSpec((tm, D), lambda i, f: (i, 0)),
        pl.BlockSpec((None, None, D, tf), lambda i, f: (layer, f, 0, 0)),
        pl.BlockSpec((1, tf), lambda i, f: (0, f)),
        pl.BlockSpec((None, tf, D), lambda i, f: (layer, f, 0)),
        _const_spec((1, D)),
        pl.BlockSpec((tm, D), lambda i, f: (i, 0)),
        _mod_spec(layer, 5, tm),
        _const_spec((1, D)), _const_spec((1, D)),
    ]
    args = [hm, w1_tiles, b1, w2, b2, x, mod, ln_g, ln_b]
    if last:
        out_specs = _group_specs(tm, D)
        out_shape = [jax.ShapeDtypeStruct((P_TOK, D), F32), jax.ShapeDtypeStruct((S_TOK, D), F32)]
    else:
        in_specs += [_mod_spec(layer + 1, 1, tm), _mod_spec(layer + 1, 0, tm)]
        args += [mod, mod]
        out_specs = [pl.BlockSpec((tm, D), lambda i, f: (i, 0))] * 2
        out_shape = [jax.ShapeDtypeStruct((N_TOK, D), F32), jax.ShapeDtypeStruct((N_TOK, D), BF16)]
    return pl.pallas_call(
        functools.partial(_ffn_kernel, nf=nf, n_p=n_p, last=last),
        grid=(N_TOK // tm, nf),
        in_specs=in_specs,
        out_specs=out_specs,
        out_shape=out_shape,
        scratch_shapes=[pltpu.VMEM((tm, D), F32)],
        compiler_params=_cparams(("arbitrary" if last else "parallel", "arbitrary")),
        name=f"ffn_l{layer}",
    )(*args)


def _rope_tile(y, cos_ref, sin_ref):
    half = DK // 2
    outs = []
    for s in range(y.shape[1] // half):
        x = y[:, s * half:(s + 1) * half]
        t = (s % 2) * half
        rot = pltpu.roll(x, half // 2, axis=1)
        outs.append(x * cos_ref[:, t:t + half] + rot * sin_ref[:, t:t + half])
    return jnp.concatenate(outs, axis=1)


def _ret_in_kernel(hm_ref, w_ref, cos_ref, sin_ref, o_ref, w_bf, *, tm, tn):
    j = pl.program_id(0)
    i = pl.program_id(1)

    @pl.when(i == 0)
    def _():
        w_bf[...] = w_ref[...].astype(BF16)

    y = _dot(hm_ref[...], w_bf[...])
    y = y * jnp.where(j < D // tn, DK ** -0.5, 1.0)
    use_rope = jnp.logical_and(j < 2 * D // tn, i >= P_TOK // tm)

    def store(t):
        for s in range(tn // DK):
            o_ref[s] = t[:, s * DK:(s + 1) * DK].astype(BF16)

    @pl.when(use_rope)
    def _():
        store(_rope_tile(y, cos_ref, sin_ref))

    @pl.when(jnp.logical_not(use_rope))
    def _():
        store(y)


def _ret_in_proj(hm, w_in, cos_t, sin_t):
    tm, tn = 1024, 1024
    n_p = P_TOK // tm
    tpos = lambda j, i: (jnp.maximum(i - n_p, 0) % (DEC_SEQ // tm), 0)
    return pl.pallas_call(
        functools.partial(_ret_in_kernel, tm=tm, tn=tn),
        grid=(6 * D // tn, N_TOK // tm),
        in_specs=[
            pl.BlockSpec((tm, D), lambda j, i: (i, 0)),
            pl.BlockSpec((D, tn), lambda j, i: (0, j)),
            pl.BlockSpec((tm, DK), tpos),
            pl.BlockSpec((tm, DK), tpos),
        ],
        out_specs=pl.BlockSpec((tn // DK, tm, DK), lambda j, i: (j, i, 0)),
        out_shape=jax.ShapeDtypeStruct((6 * D // DK, N_TOK, DK), BF16),
        scratch_shapes=[pltpu.VMEM((D, tn), BF16)],
        compiler_params=_cparams(("parallel", "arbitrary")),
        name="ret_in_proj",
    )(hm, w_in, cos_t, sin_t)


RC = 256


def _scale_rows(a, d):
    return jnp.concatenate(
        [a[:, s * LANES:(s + 1) * LANES] * d for s in range(a.shape[1] // LANES)], axis=1)


def _two_slabs(ref, rows):
    return jnp.concatenate([ref[0, rows, :], ref[1, rows, :]], axis=1)


def _retention_kernel(*refs, seq, sps, has_s0, write_state):
    refs = list(refs)
    rate_ref, q_ref, k_ref, v_ref, g_ref = [refs.pop(0) for _ in range(5)]
    s0_ref = refs.pop(0) if has_s0 else None
    o_ref = refs.pop(0)
    so_ref = refs.pop(0) if write_state else None
    mask_ref, din_ref, dout_ref, sb_all, s_all = refs
    n = seq // RC
    c_f = float(RC)

    lg = jnp.log1p(-jnp.exp2(rate_ref[...]))
    lgf, lgb = lg[0], lg[1]
    dchunk_f = jnp.exp(lgf * c_f)
    dchunk_b = jnp.exp(lgb * c_f)

    @pl.when(pl.program_id(1) == 0)
    def _():
        rows = lax.broadcasted_iota(jnp.int32, (RC, LANES), 0).astype(F32)
        diff = (lax.broadcasted_iota(jnp.int32, (RC, RC), 0)
                - lax.broadcasted_iota(jnp.int32, (RC, RC), 1)).astype(F32)
        mf = jnp.where(diff >= 0, jnp.exp(lgf[:, :RC] * jnp.maximum(diff, 0.0)), 0.0)
        mb = jnp.where(diff <= 0, jnp.exp(lgb[:, :RC] * jnp.maximum(-diff, 0.0)), 0.0)
        mask_ref[...] = mf + mb
        lf, lb = lgf[:, :LANES], lgb[:, :LANES]
        din_ref[0] = jnp.exp(lf * (rows + 1.0))
        din_ref[1] = jnp.exp(lb * (c_f - rows))
        dout_ref[0] = jnp.exp(lf * (c_f - 1.0 - rows))
        dout_ref[1] = jnp.exp(lb * rows)

    def loop(body, unroll):
        if n <= 2:
            for t in range(n):
                body(t)
        else:
            lax.fori_loop(0, n, lambda t, carry: (body(t), carry)[1], 0, unroll=unroll)

    for sq in range(sps):
        sb_ref, s_ref = sb_all.at[sq], s_all.at[sq]

        def chunk_rows(c, sq=sq):
            if isinstance(c, int):
                return pl.ds(sq * seq + c * RC, RC)
            return pl.ds(pl.multiple_of(sq * seq + c * RC, RC), RC)

        def init_state(direction, sq=sq, s_ref=s_ref):
            if has_s0:
                s_ref[...] = s0_ref[sq, direction]
            else:
                s_ref[...] = jnp.zeros((DK, DV), F32)

        def state_update(c, direction, dchunk, chunk_rows=chunk_rows, s_ref=s_ref):
            rows = chunk_rows(c)
            kd = _scale_rows(k_ref[rows, :].astype(F32), dout_ref[direction]).astype(BF16)
            upd = lax.dot_general(kd, _two_slabs(v_ref, rows), (((0,), (0,)), ((), ())),
                                  preferred_element_type=F32)
            s_ref[...] = dchunk * s_ref[...] + upd

        init_state(1)

        def pass1(t, sb_ref=sb_ref, s_ref=s_ref, state_update=state_update):
            c = n - 1 - t
            sb_ref[c] = s_ref[...].astype(BF16)
            state_update(c, 1, dchunk_b)

        loop(pass1, 1)
        if write_state:
            so_ref[sq, 1] = s_ref[...]

        init_state(0)

        def pass2(c, sb_ref=sb_ref, s_ref=s_ref, chunk_rows=chunk_rows,
                  state_update=state_update):
            rows = chunk_rows(c)
            qc = q_ref[rows, :]
            scores = lax.dot_general(qc, k_ref[rows, :], (((1,), (1,)), ((), ())),
                                     preferred_element_type=F32)
            tot = _dot((scores * mask_ref[...]).astype(BF16), _two_slabs(v_ref, rows))
            if has_s0 or not (isinstance(c, int) and c == 0):
                tot = tot + _scale_rows(_dot(qc, s_ref[...].astype(BF16)), din_ref[0])
            if has_s0 or not (isinstance(c, int) and c == n - 1):
                tot = tot + _scale_rows(_dot(qc, sb_ref[c]), din_ref[1])
            mu = jnp.mean(tot, axis=-1, keepdims=True)
            d = tot - mu
            var = jnp.mean(d * d, axis=-1, keepdims=True)
            nrm = d * lax.rsqrt(var + GN_EPS)
            gate = _silu(_two_slabs(g_ref, rows).astype(F32))
            o_ref[rows, :] = (gate * nrm).astype(BF16)
            state_update(c, 0, dchunk_f)

        loop(pass2, 2)
        if write_state:
            so_ref[sq, 0] = s_ref[...]


def _retention(rate, proj, s0, *, seq, n_seq, sps, row_block0, write_state):
    has_s0 = s0 is not None
    koff, voff, goff = D // DK, 2 * D // DV, 4 * D // DV
    rows = sps * seq
    state_block = (sps, None, 2, None, DK, DV)
    in_specs = [
        pl.BlockSpec((2, None, 1, DV), lambda h, b: (0, h, 0, 0)),
        pl.BlockSpec((None, rows, DK), lambda h, b: (h, row_block0 + b, 0)),
        pl.BlockSpec((None, rows, DK), lambda h, b: (koff + h, row_block0 + b, 0)),
        pl.BlockSpec((2, rows, DK), lambda h, b: (voff + h, row_block0 + b, 0)),
        pl.BlockSpec((2, rows, DK), lambda h, b: (goff + h, row_block0 + b, 0)),
    ]
    args = [rate, proj, proj, proj, proj]
    if has_s0:
        in_specs.append(pl.BlockSpec(state_block, lambda h, b: (b, 0, 0, h, 0, 0)))
        args.append(s0)
    out_specs = [pl.BlockSpec((rows, DV), lambda h, b: (b, h))]
    out_shape = [jax.ShapeDtypeStruct((n_seq * seq, 2 * D), BF16)]
    if write_state:
        out_specs.append(pl.BlockSpec(state_block, lambda h, b: (b, 0, 0, h, 0, 0)))
        out_shape.append(jax.ShapeDtypeStruct((n_seq, 1, 2, HEADS, DK, DV), F32))
    return pl.pallas_call(
        functools.partial(_retention_kernel, seq=seq, sps=sps, has_s0=has_s0,
                          write_state=write_state),
        grid=(HEADS, n_seq // sps),
        in_specs=in_specs,
        out_specs=out_specs,
        out_shape=out_shape,
        scratch_shapes=[pltpu.VMEM((RC, RC), F32),
                        pltpu.VMEM((2, RC, LANES), F32),
                        pltpu.VMEM((2, RC, LANES), F32),
                        pltpu.VMEM((sps, seq // RC, DK, DV), BF16),
                        pltpu.VMEM((sps, DK, DV), F32)],
        compiler_params=_cparams(("parallel", "arbitrary")),
        name=f"retention_t{seq}",
    )(*args)


RET_SLAB_TN = 1024


def _ret_w_slabs_kernel(w_ref, o_ref):
    for s in range(RET_SLAB_TN // DK):
        o_ref[s] = w_ref[:, s * DK:(s + 1) * DK].astype(BF16)


def _ret_w_slabs(w_in):
    return pl.pallas_call(
        _ret_w_slabs_kernel,
        grid=(6 * D // RET_SLAB_TN,),
        in_specs=[pl.BlockSpec((D, RET_SLAB_TN), lambda j: (0, j))],
        out_specs=pl.BlockSpec((RET_SLAB_TN // DK, D, DK), lambda j: (j, 0, 0)),
        out_shape=jax.ShapeDtypeStruct((6 * D // DK, D, DK), BF16),
        compiler_params=_cparams(("parallel",)),
        name="ret_w_slabs",
    )(w_in)


def _rope_rows(y, cos, sin):
    half = DK // 2
    outs = []
    for s in range(2):
        x = y[:, s * half:(s + 1) * half]
        rot = pltpu.roll(x, half // 2, axis=1)
        outs.append(x * cos[:, s * half:(s + 1) * half] + rot * sin[:, s * half:(s + 1) * half])
    return jnp.concatenate(outs, axis=1)


def _ret_mixer_kernel(*refs, seq, sps, latent, has_s0, write_state):
    refs = list(refs)
    rate_ref, hm_ref, wq_ref, wk_ref, wv_ref, wg_ref = [refs.pop(0) for _ in range(6)]
    cos_ref, sin_ref = (refs.pop(0), refs.pop(0)) if latent else (None, None)
    s0_ref = refs.pop(0) if has_s0 else None
    o_ref = refs.pop(0)
    so_ref = refs.pop(0) if write_state else None
    mask_ref, din_ref, dout_ref, sb_all, s_all, k_s, v_s = refs
    n = seq // RC
    c_f = float(RC)

    lg = jnp.log1p(-jnp.exp2(rate_ref[...]))
    lgf, lgb = lg[0], lg[1]
    dchunk_f = jnp.exp(lgf * c_f)
    dchunk_b = jnp.exp(lgb * c_f)
    rows_f = lax.broadcasted_iota(jnp.int32, (RC, LANES), 0).astype(F32)
    diff = (lax.broadcasted_iota(jnp.int32, (RC, RC), 0)
            - lax.broadcasted_iota(jnp.int32, (RC, RC), 1)).astype(F32)
    mf = jnp.where(diff >= 0, jnp.exp(lgf[:, :RC] * jnp.maximum(diff, 0.0)), 0.0)
    mb = jnp.where(diff <= 0, jnp.exp(lgb[:, :RC] * jnp.maximum(-diff, 0.0)), 0.0)
    mask_ref[...] = mf + mb
    lf, lb = lgf[:, :LANES], lgb[:, :LANES]
    din_ref[0] = jnp.exp(lf * (rows_f + 1.0))
    din_ref[1] = jnp.exp(lb * (c_f - rows_f))
    dout_ref[0] = jnp.exp(lf * (c_f - 1.0 - rows_f))
    dout_ref[1] = jnp.exp(lb * rows_f)

    def loop(body, unroll):
        if n <= 2:
            for t in range(n):
                body(t)
        else:
            lax.fori_loop(0, n, lambda t, carry: (body(t), carry)[1], 0, unroll=unroll)

    def rope(y, c):
        if not latent:
            return y
        pos = pl.ds(c * RC, RC) if isinstance(c, int) else pl.ds(pl.multiple_of(c * RC, RC), RC)
        return _rope_rows(y, cos_ref[pos, :], sin_ref[pos, :])

    def two(w_ref, x):
        return jnp.concatenate([_dot(x, w_ref[0]), _dot(x, w_ref[1])], axis=1)

    for sq in range(sps):
        sb_ref, s_ref = sb_all.at[sq], s_all.at[sq]

        def chunk_rows(c, sq=sq):
            if isinstance(c, int):
                return pl.ds(sq * seq + c * RC, RC)
            return pl.ds(pl.multiple_of(sq * seq + c * RC, RC), RC)

        def init_state(direction, sq=sq, s_ref=s_ref):
            if has_s0:
                s_ref[...] = s0_ref[sq, direction]
            else:
                s_ref[...] = jnp.zeros((DK, DV), F32)

        def state_update(c, direction, dchunk, chunk_rows=chunk_rows, s_ref=s_ref):
            rows = chunk_rows(c)
            kd = _scale_rows(k_s[rows, :].astype(F32), dout_ref[direction]).astype(BF16)
            upd = lax.dot_general(kd, v_s[rows, :], (((0,), (0,)), ((), ())),
                                  preferred_element_type=F32)
            s_ref[...] = dchunk * s_ref[...] + upd

        def phase0(c, chunk_rows=chunk_rows):
            rows = chunk_rows(c)
            x = hm_ref[rows, :]
            k_s[rows, :] = rope(_dot(x, wk_ref[...]), c).astype(BF16)
            v_s[rows, :] = two(wv_ref, x).astype(BF16)

        loop(phase0, 2)

        init_state(1)

        def phase1(t, sb_ref=sb_ref, s_ref=s_ref, state_update=state_update):
            c = n - 1 - t
            sb_ref[c] = s_ref[...].astype(BF16)
            state_update(c, 1, dchunk_b)

        loop(phase1, 1)
        if write_state:
            so_ref[sq, 1] = s_ref[...]

        init_state(0)

        def phase2(c, sb_ref=sb_ref, s_ref=s_ref, chunk_rows=chunk_rows,
                   state_update=state_update):
            rows = chunk_rows(c)
            x = hm_ref[rows, :]
            qc = rope(_dot(x, wq_ref[...]) * DK ** -0.5, c).astype(BF16)
            gate = _silu(two(wg_ref, x))
            vc = v_s[rows, :]
            scores = lax.dot_general(qc, k_s[rows, :], (((1,), (1,)), ((), ())),
                                     preferred_element_type=F32)
            tot = _dot((scores * mask_ref[...]).astype(BF16), vc)
            if has_s0 or not (isinstance(c, int) and c == 0):
                tot = tot + _scale_rows(_dot(qc, s_ref[...].astype(BF16)), din_ref[0])
            if has_s0 or not (isinstance(c, int) and c == n - 1):
                tot = tot + _scale_rows(_dot(qc, sb_ref[c]), din_ref[1])
            mu = jnp.mean(tot, axis=-1, keepdims=True)
            d = tot - mu
            var = jnp.mean(d * d, axis=-1, keepdims=True)
            nrm = d * lax.rsqrt(var + GN_EPS)
            o_ref[rows, :] = (gate * nrm).astype(BF16)
            state_update(c, 0, dchunk_f)

        loop(phase2, 2)
        if write_state:
            so_ref[sq, 0] = s_ref[...]


def _ret_mixer(rate, hm, w_slabs, tables, s0, *, seq, n_seq, sps, row_block0, write_state):
    has_s0 = s0 is not None
    latent = tables is not None
    koff, voff, goff = D // DK, 2 * D // DV, 4 * D // DV
    rows = sps * seq
    state_block = (sps, None, 2, None, DK, DV)
    in_specs = [
        pl.BlockSpec((2, None, 1, DV), lambda b, h: (0, h, 0, 0)),
        pl.BlockSpec((rows, D), lambda b, h: (row_block0 + b, 0)),
        pl.BlockSpec((None, D, DK), lambda b, h: (h, 0, 0)),
        pl.BlockSpec((None, D, DK), lambda b, h: (koff + h, 0, 0)),
        pl.BlockSpec((2, D, DK), lambda b, h: (voff + h, 0, 0)),
        pl.BlockSpec((2, D, DK), lambda b, h: (goff + h, 0, 0)),
    ]
    args = [rate, hm, w_slabs, w_slabs, w_slabs, w_slabs]
    if latent:
        in_specs += [_const_spec((DEC_SEQ, DK)), _const_spec((DEC_SEQ, DK))]
        args += list(tables)
    if has_s0:
        in_specs.append(pl.BlockSpec(state_block, lambda b, h: (b, 0, 0, h, 0, 0)))
        args.append(s0)
    out_specs = [pl.BlockSpec((rows, DV), lambda b, h: (b, h))]
    out_shape = [jax.ShapeDtypeStruct((n_seq * seq, 2 * D), BF16)]
    if write_state:
        out_specs.append(pl.BlockSpec(state_block, lambda b, h: (b, 0, 0, h, 0, 0)))
        out_shape.append(jax.ShapeDtypeStruct((n_seq, 1, 2, HEADS, DK, DV), F32))
    return pl.pallas_call(
        functools.partial(_ret_mixer_kernel, seq=seq, sps=sps, latent=latent, has_s0=has_s0,
                          write_state=write_state),
        grid=(n_seq // sps, HEADS),
        in_specs=in_specs,
        out_specs=out_specs,
        out_shape=out_shape,
        scratch_shapes=[pltpu.VMEM((RC, RC), F32),
                        pltpu.VMEM((2, RC, LANES), F32),
                        pltpu.VMEM((2, RC, LANES), F32),
                        pltpu.VMEM((sps, seq // RC, DK, DV), BF16),
                        pltpu.VMEM((sps, DK, DV), F32),
                        pltpu.VMEM((rows, DK), BF16),
                        pltpu.VMEM((rows, DV), BF16)],
        compiler_params=_cparams(("parallel", "parallel")),
        name=f"ret_mixer_t{seq}",
    )(*args)


def _rope_tables():
    t = jnp.arange(DEC_SEQ)
    row = (t // GRID_W).astype(F32)
    col = (t % GRID_W).astype(F32)
    quarter = DK // 4
    inv = ROPE_BASE ** (-jnp.arange(quarter, dtype=F32) / quarter)
    ang_r = row[:, None] * inv
    ang_c = col[:, None] * inv
    cos_t = jnp.concatenate([jnp.cos(ang_r)] * 2 + [jnp.cos(ang_c)] * 2, axis=1)
    sin_t = jnp.concatenate([-jnp.sin(ang_r), jnp.sin(ang_r), -jnp.sin(ang_c), jnp.sin(ang_c)], axis=1)
    return cos_t, sin_t


def kernel(x_prompt, x_sample, state_ret, c, c_ctx, w_mod, b_mod, ln1_g, ln1_b, ln2_g, ln2_b,
           w_pw1, b_pw1, w_dw, b_dw, cn_g, cn_b, w_pw2, b_pw2,
           w_ret_in, ret_log2_rate, w_ret_o, w_ff1, b_ff1, w_ff2, b_ff2):
    xp = x_prompt.reshape(P_TOK, D)
    xs = x_sample.reshape(S_TOK, D)
    cond = jnp.concatenate(
        [c_ctx[None, :], c, jnp.zeros((N_COND - 1 - DEC_BATCH, D), F32)], axis=0)

    mod = _modulation(cond, w_mod, b_mod)
    mod = mod.reshape(DEPTH, N_COND, 6, D).transpose(0, 2, 1, 3).reshape(DEPTH * 6 * N_COND, 1, D)

    bf = lambda w: w.astype(BF16)
    row = lambda v: v.reshape(1, -1)

    x, hm = _conv_mixer(xp, xs, mod, bf(w_pw1[0]), row(b_pw1[0]), w_dw[0], row(b_dw[0]),
                        row(cn_g[0]), row(cn_b[0]), bf(w_pw2[0]), row(b_pw2[0]),
                        row(ln1_g[0]), row(ln1_b[0]))
    w1_tiles = _ffn_w1_tiles(w_ff1)
    w2_bf = bf(w_ff2)
    x, hm = _ffn(hm, w1_tiles, row(b_ff1[0]), w2_bf, row(b_ff2[0]),
                 x, mod, 0, row(ln2_g[0]), row(ln2_b[0]), last=False)

    cos_t, sin_t = _rope_tables()
    w_slabs = _ret_w_slabs(w_ret_in[0])
    rate = jnp.broadcast_to(ret_log2_rate[0][:, :, None, None], (2, HEADS, 1, DV))
    gp, new_state = _ret_mixer(rate, hm, w_slabs, None, None, seq=SEQ, n_seq=BATCH, sps=4,
                               row_block0=0, write_state=True)
    (gs,) = _ret_mixer(rate, hm, w_slabs, (cos_t, sin_t), state_ret, seq=DEC_SEQ,
                       n_seq=DEC_BATCH, sps=1, row_block0=P_TOK // DEC_SEQ, write_state=False)
    x, hm = _proj_res_ln([gp, gs], bf(w_ret_o[0]), jnp.zeros((1, D), F32), [x], mod, 1,
                         row(ln1_g[1]), row(ln1_b[1]), tm=256, sub=128)
    yp, ys = _ffn(hm, w1_tiles, row(b_ff1[1]), w2_bf, row(b_ff2[1]),
                  x, mod, 1, row(ln2_g[1]), row(ln2_b[1]), last=True)

    return (yp.reshape(BATCH, SEQ, D), ys.reshape(DEC_BATCH, DEC_SEQ, D), new_state)
```

```python
import functools

import jax
import jax.numpy as jnp
from jax import lax
from jax.experimental import pallas as pl
from jax.experimental.pallas import tpu as pltpu

F32 = jnp.float32
BF16 = jnp.bfloat16

D = 2048
BATCH = 16
SEQ = 256
DEC_BATCH = 4
DEC_SEQ = 2048
GRID_W = 64
CONV_WIDTH = 31
D_FF = 4 * D
HEADS = 8
DK = D // HEADS
DV = 2 * D // HEADS
ROPE_BASE = 10000.0
LN_EPS = 1e-5
GN_EPS = 1e-5
DEPTH = 2
ALPHA = (2.0 * DEPTH) ** 0.25

P_TOK = BATCH * SEQ
S_TOK = DEC_BATCH * DEC_SEQ
N_TOK = P_TOK + S_TOK
N_COND = 8

LANES = 128
SUBLANES = 8
TM = 512
VMEM_LIMIT = 60 * 1024 * 1024


def _cparams(sem):
    return pltpu.CompilerParams(dimension_semantics=sem, vmem_limit_bytes=VMEM_LIMIT)


def _cond_index(i, tm):
    r = i * tm
    return jnp.where(r < P_TOK, 0, 1 + (r - P_TOK) // DEC_SEQ)


def _mod_spec(layer, part, tm, row_axis=0):
    base = (layer * 6 + part) * N_COND
    return pl.BlockSpec((None, 1, D),
                        lambda *ids: (base + _cond_index(ids[row_axis], tm), 0, 0))


def _const_spec(shape):
    return pl.BlockSpec(shape, lambda *ids: (0,) * len(shape))


def _group_specs(tm, cols, row_axis=0):
    n_p = P_TOK // tm
    ctx = pl.BlockSpec((tm, cols), lambda *ids: (jnp.minimum(ids[row_axis], n_p - 1), 0))
    lat = pl.BlockSpec((tm, cols), lambda *ids: (jnp.maximum(ids[row_axis] - n_p, 0), 0))
    return [ctx, lat]


def _layer_norm(r, g, b):
    mu = jnp.mean(r, axis=-1, keepdims=True)
    d = r - mu
    var = jnp.mean(d * d, axis=-1, keepdims=True)
    return d * lax.rsqrt(var + LN_EPS) * g + b


def _silu(x):
    return x * jax.nn.sigmoid(x)


def _dot(a, b):
    return jnp.dot(a, b, preferred_element_type=F32)


def _mod_kernel(cond_ref, w_ref, b_ref, o_ref):
    s = _silu(cond_ref[...]).astype(BF16)
    o_ref[...] = _dot(s, w_ref[...].astype(BF16)) + b_ref[...]


def _modulation(cond, w_mod, b_mod):
    tn = 1024
    return pl.pallas_call(
        _mod_kernel,
        grid=(DEPTH, 6 * D // tn),
        in_specs=[
            pl.BlockSpec((N_COND, D), lambda l, j: (0, 0)),
            pl.BlockSpec((None, D, tn), lambda l, j: (l, 0, j)),
            pl.BlockSpec((None, 1, tn), lambda l, j: (l, 0, j)),
        ],
        out_specs=pl.BlockSpec((None, N_COND, tn), lambda l, j: (l, 0, j)),
        out_shape=jax.ShapeDtypeStruct((DEPTH, N_COND, 6 * D), F32),
        compiler_params=_cparams(("parallel", "parallel")),
        name="modulation",
    )(cond, w_mod, b_mod.reshape(DEPTH, 1, 6 * D))


CONV_TB = 256
CONV_HALO = 16
CONV_WIN = CONV_TB + 2 * CONV_HALO
CONV_CW = 256
CONV_RB = 32
CONV_SUB = 128


def _dw_taps(win_ref, sh_ref, wb_ref, y_ref, c0):
    shift = CONV_HALO - CONV_WIDTH // 2
    cols = slice(c0, c0 + CONV_CW)
    w0 = win_ref[:, cols]
    for s in range(1, SUBLANES):
        sh_ref[s - 1] = pltpu.roll(w0, CONV_WIN - s, axis=0)
    groups = CONV_RB // SUBLANES
    for r in range(0, CONV_TB, CONV_RB):
        accs = [jnp.zeros((SUBLANES, CONV_CW), F32)] * groups
        for k in range(CONV_WIDTH):
            off = shift + k
            s, base = off % SUBLANES, r + off - off % SUBLANES
            if s == 0:
                src = win_ref[base:base + CONV_RB, cols]
            else:
                src = sh_ref[s - 1, base:base + CONV_RB, :]
            wk = wb_ref[k, :, cols]
            accs = [a + src[g * SUBLANES:(g + 1) * SUBLANES] * wk for g, a in enumerate(accs)]
        y_ref[r:r + CONV_RB, cols] = jnp.concatenate(accs, axis=0)


def _conv_mixer_kernel(xp_ref, xs_ref, prev_ref, next_ref, sc1_ref, sh1_ref, gate_ref,
                       sc2_ref, sh2_ref, w1_ref, b1_ref, wdw_ref, bdw_ref, cng_ref, cnb_ref,
                       w2_ref, b2_ref, lng_ref, lnb_ref, xo_ref, hmo_ref,
                       hm_ref, win_ref, sh_ref, wb_ref, y_ref):
    i = pl.program_id(0)

    @pl.when(i == 0)
    def _():
        for k in range(CONV_WIDTH):
            wb_ref[k] = jnp.broadcast_to(wdw_ref[pl.ds(k, 1), :], (SUBLANES, D))

    n_p = P_TOK // CONV_TB
    per_seq = DEC_SEQ // CONV_TB
    pos = (i - n_p) % per_seq
    latent = i >= n_p
    has_prev = jnp.logical_and(latent, pos != 0)
    has_next = jnp.logical_and(latent, pos != per_seq - 1)
    lo, hi = CONV_HALO, CONV_HALO + CONV_TB

    def modulate(x):
        return (x * (1.0 + sc1_ref[...]) + sh1_ref[...]).astype(BF16)

    hm_ref[0:lo, :] = modulate(prev_ref[...])
    hm_ref[lo:hi, :] = modulate(jnp.where(latent, xs_ref[...], xp_ref[...]))
    hm_ref[hi:, :] = modulate(next_ref[...])

    for c0 in range(0, D, CONV_CW):
        hm = hm_ref[...]
        a = _dot(hm, w1_ref[:, c0:c0 + CONV_CW]) + b1_ref[:, c0:c0 + CONV_CW]
        g = _dot(hm, w1_ref[:, D + c0:D + c0 + CONV_CW]) + b1_ref[:, D + c0:D + c0 + CONV_CW]
        u = a * jax.nn.sigmoid(g)
        win_ref[0:lo, c0:c0 + CONV_CW] = jnp.where(has_prev, u[0:lo], 0.0)
        win_ref[lo:hi, c0:c0 + CONV_CW] = u[lo:hi]
        win_ref[hi:, c0:c0 + CONV_CW] = jnp.where(has_next, u[hi:], 0.0)
        _dw_taps(win_ref, sh_ref, wb_ref, y_ref, c0)

    v = _silu(_layer_norm(y_ref[...] + bdw_ref[...], cng_ref[...], cnb_ref[...])).astype(BF16)
    for s in range(CONV_TB // CONV_SUB):
        rows = slice(s * CONV_SUB, (s + 1) * CONV_SUB)
        y2 = _dot(v[rows], w2_ref[...]) + b2_ref[...]
        x = jnp.where(latent, xs_ref[rows, :], xp_ref[rows, :])
        xn = _layer_norm(ALPHA * x + gate_ref[...] * y2, lng_ref[...], lnb_ref[...])
        xo_ref[rows, :] = xn
        hmo_ref[rows, :] = (xn * (1.0 + sc2_ref[...]) + sh2_ref[...]).astype(BF16)


def _conv_mixer(xp, xs, mod, w_pw1, b_pw1, w_dw, b_dw, cn_g, cn_b, w_pw2, b_pw2, ln_g, ln_b):
    tb = CONV_TB
    n_p = P_TOK // tb
    hb = tb // CONV_HALO
    last_halo = S_TOK // CONV_HALO - 1
    vec = _const_spec((1, D))
    resident = lambda shape: pl.BlockSpec(shape, lambda i: (0, 0), pipeline_mode=pl.Buffered(1))
    return pl.pallas_call(
        _conv_mixer_kernel,
        grid=(N_TOK // tb,),
        in_specs=_group_specs(tb, D) + [
            pl.BlockSpec((CONV_HALO, D), lambda i: (jnp.clip((i - n_p) * hb - 1, 0, last_halo), 0)),
            pl.BlockSpec((CONV_HALO, D), lambda i: (jnp.clip((i - n_p + 1) * hb, 0, last_halo), 0)),
            _mod_spec(0, 1, tb), _mod_spec(0, 0, tb), _mod_spec(0, 2, tb),
            _mod_spec(0, 4, tb), _mod_spec(0, 3, tb),
            resident((D, 2 * D)), _const_spec((1, 2 * D)),
            _const_spec((CONV_WIDTH, D)), vec, vec, vec,
            resident((D, D)), vec, vec, vec,
        ],
        out_specs=[pl.BlockSpec((tb, D), lambda i: (i, 0)),
                   pl.BlockSpec((tb, D), lambda i: (i, 0))],
        out_shape=[jax.ShapeDtypeStruct((N_TOK, D), F32),
                   jax.ShapeDtypeStruct((N_TOK, D), BF16)],
        scratch_shapes=[pltpu.VMEM((CONV_WIN, D), BF16),
                        pltpu.VMEM((CONV_WIN, D), F32),
                        pltpu.VMEM((SUBLANES - 1, CONV_WIN, CONV_CW), F32),
                        pltpu.VMEM((CONV_WIDTH, SUBLANES, D), F32),
                        pltpu.VMEM((CONV_TB, D), F32)],
        compiler_params=_cparams(("arbitrary",)),
        name="conv_mixer",
    )(xp, xs, xs, xs, mod, mod, mod, mod, mod, w_pw1, b_pw1, w_dw, b_dw, cn_g, cn_b,
      w_pw2, b_pw2, ln_g, ln_b)


EPI_SUB = 256


def _proj_res_ln_kernel(ap_ref, as_ref, w_ref, x_ref, gate_ref, lng_ref, lnb_ref, sc_ref, sh_ref,
                        xo_ref, hmo_ref, *, n_p, tm, sub):
    i = pl.program_id(0)
    for s in range(tm // sub):
        rows = pl.ds(s * sub, sub)
        y = _dot(jnp.where(i < n_p, ap_ref[rows, :], as_ref[rows, :]), w_ref[...])
        xn = _layer_norm(ALPHA * x_ref[rows, :] + gate_ref[...] * y, lng_ref[...], lnb_ref[...])
        xo_ref[rows, :] = xn
        hmo_ref[rows, :] = (xn * (1.0 + sc_ref[...]) + sh_ref[...]).astype(BF16)


def _proj_res_ln(a_ctx, a_lat, w, x, mod, layer, ln_g, ln_b, tm, sub):
    kdim = w.shape[0]
    return pl.pallas_call(
        functools.partial(_proj_res_ln_kernel, n_p=P_TOK // tm, tm=tm, sub=sub),
        grid=(N_TOK // tm,),
        in_specs=_group_specs(tm, kdim) + [
            pl.BlockSpec((kdim, D), lambda i: (0, 0), pipeline_mode=pl.Buffered(1)),
            pl.BlockSpec((tm, D), lambda i: (i, 0)),
            _mod_spec(layer, 2, tm), _const_spec((1, D)), _const_spec((1, D)),
            _mod_spec(layer, 4, tm), _mod_spec(layer, 3, tm)],
        out_specs=[pl.BlockSpec((tm, D), lambda i: (i, 0)),
                   pl.BlockSpec((tm, D), lambda i: (i, 0))],
        out_shape=[jax.ShapeDtypeStruct((N_TOK, D), F32),
                   jax.ShapeDtypeStruct((N_TOK, D), BF16)],
        compiler_params=_cparams(("parallel",)),
        name=f"proj_res_ln_l{layer}",
    )(a_ctx, a_lat, w, x, mod, ln_g, ln_b, mod, mod)


def _ffn_kernel(*refs, nf, n_p, last):
    hm_ref, w1_ref, b1_ref, w2_ref, b2_ref, x_ref, gate_ref, lng_ref, lnb_ref = refs[:9]
    if last:
        yp_ref, ys_ref, acc_ref = refs[9:]
    else:
        sc_ref, sh_ref, xo_ref, hmo_ref, acc_ref = refs[9:]
    i = pl.program_id(0)
    f = pl.program_id(1)
    tm = hm_ref.shape[0]

    h = _dot(hm_ref[...], w1_ref[...]) + b1_ref[...]
    h = jnp.square(jnp.maximum(h, 0.0)).astype(BF16)

    @pl.when(f == 0)
    def _():
        acc_ref[...] = _dot(h, w2_ref[...])

    @pl.when(jnp.logical_and(f > 0, f < nf - 1))
    def _():
        acc_ref[...] += _dot(h, w2_ref[...])

    def finish(store):
        for s in range(tm // EPI_SUB):
            rows = pl.ds(s * EPI_SUB, EPI_SUB)
            y = (acc_ref[rows, :] + _dot(h[s * EPI_SUB:(s + 1) * EPI_SUB], w2_ref[...])
                 + b2_ref[...])
            store(rows, _layer_norm(ALPHA * x_ref[rows, :] + gate_ref[...] * y,
                                    lng_ref[...], lnb_ref[...]))

    def store_ctx(rows, xn):
        yp_ref[rows, :] = xn

    def store_lat(rows, xn):
        ys_ref[rows, :] = xn

    def store_next(rows, xn):
        xo_ref[rows, :] = xn
        hmo_ref[rows, :] = (xn * (1.0 + sc_ref[...]) + sh_ref[...]).astype(BF16)

    is_last_f = f == nf - 1
    if last:
        pl.when(jnp.logical_and(is_last_f, i < n_p))(lambda: finish(store_ctx))
        pl.when(jnp.logical_and(is_last_f, i >= n_p))(lambda: finish(store_lat))
    else:
        pl.when(is_last_f)(lambda: finish(store_next))


FFN_TF = 1024


def _w1_tiles_kernel(w_ref, o_ref):
    o_ref[...] = w_ref[...].astype(BF16)


def _ffn_w1_tiles(w_ff1):
    nf = D_FF // FFN_TF
    return pl.pallas_call(
        _w1_tiles_kernel,
        grid=(DEPTH, nf),
        in_specs=[pl.BlockSpec((None, D, FFN_TF), lambda l, f: (l, 0, f))],
        out_specs=pl.BlockSpec((None, None, D, FFN_TF), lambda l, f: (l, f, 0, 0)),
        out_shape=jax.ShapeDtypeStruct((DEPTH, nf, D, FFN_TF), BF16),
        compiler_params=_cparams(("parallel", "parallel")),
        name="ffn_w1_tiles",
    )(w_ff1)


def _ffn(hm, w1_tiles, b1, w2, b2, x, mod, layer, ln_g, ln_b, last):
    tm, tf = TM, FFN_TF
    nf = D_FF // tf
    n_p = P_TOK // tm
    in_specs = [
        pl.BlockSpec((tm, D), lambda i, f: (i, 0)),
        pl.BlockSpec((None, None, D, tf), lambda i, f: (layer, f, 0, 0)),
        pl.BlockSpec((1, tf), lambda i, f: (0, f)),
        pl.BlockSpec((None, tf, D), lambda i, f: (layer, f, 0)),
        _const_spec((1, D)),
        pl.BlockSpec((tm, D), lambda i, f: (i, 0)),
        _mod_spec(layer, 5, tm),
        _const_spec((1, D)), _const_spec((1, D)),
    ]
    args = [hm, w1_tiles, b1, w2, b2, x, mod, ln_g, ln_b]
    if last:
        out_specs = _group_specs(tm, D)
        out_shape = [jax.ShapeDtypeStruct((P_TOK, D), F32), jax.ShapeDtypeStruct((S_TOK, D), F32)]
    else:
        in_specs += [_mod_spec(layer + 1, 1, tm), _mod_spec(layer + 1, 0, tm)]
        args += [mod, mod]
        out_specs = [pl.BlockSpec((tm, D), lambda i, f: (i, 0))] * 2
        out_shape = [jax.ShapeDtypeStruct((N_TOK, D), F32), jax.ShapeDtypeStruct((N_TOK, D), BF16)]
    return pl.pallas_call(
        functools.partial(_ffn_kernel, nf=nf, n_p=n_p, last=last),
        grid=(N_TOK // tm, nf),
        in_specs=in_specs,
        out_specs=out_specs,
        out_shape=out_shape,
        scratch_shapes=[pltpu.VMEM((tm, D), F32)],
        compiler_params=_cparams(("arbitrary" if last else "parallel", "arbitrary")),
        name=f"ffn_l{layer}",
    )(*args)


RC = 256
RET_SLAB_TN = 1024


def _scale_rows(a, d):
    return jnp.concatenate(
        [a[:, s * LANES:(s + 1) * LANES] * d for s in range(a.shape[1] // LANES)], axis=1)


def _ret_w_slabs_kernel(w_ref, o_ref):
    for s in range(RET_SLAB_TN // DK):
        o_ref[s] = w_ref[:, s * DK:(s + 1) * DK].astype(BF16)


def _ret_w_slabs(w_in):
    return pl.pallas_call(
        _ret_w_slabs_kernel,
        grid=(6 * D // RET_SLAB_TN,),
        in_specs=[pl.BlockSpec((D, RET_SLAB_TN), lambda j: (0, j))],
        out_specs=pl.BlockSpec((RET_SLAB_TN // DK, D, DK), lambda j: (j, 0, 0)),
        out_shape=jax.ShapeDtypeStruct((6 * D // DK, D, DK), BF16),
        compiler_params=_cparams(("parallel",)),
        name="ret_w_slabs",
    )(w_in)


def _rope_rows(y, cos, sin):
    half = DK // 2
    outs = []
    for s in range(2):
        x = y[:, s * half:(s + 1) * half]
        rot = pltpu.roll(x, half // 2, axis=1)
        outs.append(x * cos[:, s * half:(s + 1) * half] + rot * sin[:, s * half:(s + 1) * half])
    return jnp.concatenate(outs, axis=1)


def _ret_mixer_kernel(*refs, seq, sps, latent, has_s0, write_state):
    refs = list(refs)
    rate_ref, hm_ref, wq_ref, wk_ref, wv_ref, wg_ref = [refs.pop(0) for _ in range(6)]
    cos_ref, sin_ref = (refs.pop(0), refs.pop(0)) if latent else (None, None)
    s0_ref = refs.pop(0) if has_s0 else None
    o_ref = refs.pop(0)
    so_ref = refs.pop(0) if write_state else None
    mask_ref, din_ref, dout_ref, sb_all, s_all, k_s, v_s = refs
    n = seq // RC
    c_f = float(RC)

    lg = jnp.log1p(-jnp.exp2(rate_ref[...]))
    lgf, lgb = lg[0], lg[1]
    dchunk_f = jnp.exp(lgf * c_f)
    dchunk_b = jnp.exp(lgb * c_f)
    rows_f = lax.broadcasted_iota(jnp.int32, (RC, LANES), 0).astype(F32)
    diff = (lax.broadcasted_iota(jnp.int32, (RC, RC), 0)
            - lax.broadcasted_iota(jnp.int32, (RC, RC), 1)).astype(F32)
    mf = jnp.where(diff >= 0, jnp.exp(lgf[:, :RC] * jnp.maximum(diff, 0.0)), 0.0)
    mb = jnp.where(diff <= 0, jnp.exp(lgb[:, :RC] * jnp.maximum(-diff, 0.0)), 0.0)
    mask_ref[...] = mf + mb
    lf, lb = lgf[:, :LANES], lgb[:, :LANES]
    din_ref[0] = jnp.exp(lf * (rows_f + 1.0))
    din_ref[1] = jnp.exp(lb * (c_f - rows_f))
    dout_ref[0] = jnp.exp(lf * (c_f - 1.0 - rows_f))
    dout_ref[1] = jnp.exp(lb * rows_f)

    def loop(body, unroll):
        if n <= 2:
            for t in range(n):
                body(t)
        else:
            lax.fori_loop(0, n, lambda t, carry: (body(t), carry)[1], 0, unroll=unroll)

    def rope(y, c):
        if not latent:
            return y
        pos = pl.ds(c * RC, RC) if isinstance(c, int) else pl.ds(pl.multiple_of(c * RC, RC), RC)
        return _rope_rows(y, cos_ref[pos, :], sin_ref[pos, :])

    def two(w_ref, x):
        return jnp.concatenate([_dot(x, w_ref[0]), _dot(x, w_ref[1])], axis=1)

    for sq in range(sps):
        sb_ref, s_ref = sb_all.at[sq], s_all.at[sq]

        def chunk_rows(c, sq=sq):
            if isinstance(c, int):
                return pl.ds(sq * seq + c * RC, RC)
            return pl.ds(pl.multiple_of(sq * seq + c * RC, RC), RC)

        def init_state(direction, sq=sq, s_ref=s_ref):
            if has_s0:
                s_ref[...] = s0_ref[sq, direction]
            else:
                s_ref[...] = jnp.zeros((DK, DV), F32)

        def state_update(c, direction, dchunk, chunk_rows=chunk_rows, s_ref=s_ref):
            rows = chunk_rows(c)
            kd = _scale_rows(k_s[rows, :].astype(F32), dout_ref[direction]).astype(BF16)
            upd = lax.dot_general(kd, v_s[rows, :], (((0,), (0,)), ((), ())),
                                  preferred_element_type=F32)
            s_ref[...] = dchunk * s_ref[...] + upd

        def phase0(c, chunk_rows=chunk_rows):
            rows = chunk_rows(c)
            x = hm_ref[rows, :]
            k_s[rows, :] = rope(_dot(x, wk_ref[...]), c).astype(BF16)
            v_s[rows, :] = two(wv_ref, x).astype(BF16)

        loop(phase0, 2)

        init_state(1)

        def phase1(t, sb_ref=sb_ref, s_ref=s_ref, state_update=state_update):
            c = n - 1 - t
            sb_ref[c] = s_ref[...].astype(BF16)
            state_update(c, 1, dchunk_b)

        loop(phase1, 4)
        if write_state:
            so_ref[sq, 1] = s_ref[...]

        init_state(0)

        def phase2(c, sb_ref=sb_ref, s_ref=s_ref, chunk_rows=chunk_rows,
                   state_update=state_update):
            rows = chunk_rows(c)
            x = hm_ref[rows, :]
            qc = rope(_dot(x, wq_ref[...]) * DK ** -0.5, c).astype(BF16)
            gate = _silu(two(wg_ref, x))
            vc = v_s[rows, :]
            scores = lax.dot_general(qc, k_s[rows, :], (((1,), (1,)), ((), ())),
                                     preferred_element_type=F32)
            tot = _dot((scores * mask_ref[...]).astype(BF16), vc)
            if has_s0 or not (isinstance(c, int) and c == 0):
                tot = tot + _scale_rows(_dot(qc, s_ref[...].astype(BF16)), din_ref[0])
            if has_s0 or not (isinstance(c, int) and c == n - 1):
                tot = tot + _scale_rows(_dot(qc, sb_ref[c]), din_ref[1])
            mu = jnp.mean(tot, axis=-1, keepdims=True)
            d = tot - mu
            var = jnp.mean(d * d, axis=-1, keepdims=True)
            nrm = d * lax.rsqrt(var + GN_EPS)
            o_ref[rows, :] = (gate * nrm).astype(BF16)
            state_update(c, 0, dchunk_f)

        loop(phase2, 2)
        if write_state:
            so_ref[sq, 0] = s_ref[...]


def _ret_mixer(rate, hm, w_slabs, tables, s0, *, seq, n_seq, sps, row_block0, write_state):
    has_s0 = s0 is not None
    latent = tables is not None
    koff, voff, goff = D // DK, 2 * D // DV, 4 * D // DV
    rows = sps * seq
    state_block = (sps, None, 2, None, DK, DV)
    in_specs = [
        pl.BlockSpec((2, None, 1, DV), lambda b, h: (0, h, 0, 0)),
        pl.BlockSpec((rows, D), lambda b, h: (row_block0 + b, 0)),
        pl.BlockSpec((None, D, DK), lambda b, h: (h, 0, 0)),
        pl.BlockSpec((None, D, DK), lambda b, h: (koff + h, 0, 0)),
        pl.BlockSpec((2, D, DK), lambda b, h: (voff + h, 0, 0)),
        pl.BlockSpec((2, D, DK), lambda b, h: (goff + h, 0, 0)),
    ]
    args = [rate, hm, w_slabs, w_slabs, w_slabs, w_slabs]
    if latent:
        in_specs += [_const_spec((DEC_SEQ, DK)), _const_spec((DEC_SEQ, DK))]
        args += list(tables)
    if has_s0:
        in_specs.append(pl.BlockSpec(state_block, lambda b, h: (b, 0, 0, h, 0, 0)))
        args.append(s0)
    out_specs = [pl.BlockSpec((rows, DV), lambda b, h: (b, h))]
    out_shape = [jax.ShapeDtypeStruct((n_seq * seq, 2 * D), BF16)]
    if write_state:
        out_specs.append(pl.BlockSpec(state_block, lambda b, h: (b, 0, 0, h, 0, 0)))
        out_shape.append(jax.ShapeDtypeStruct((n_seq, 1, 2, HEADS, DK, DV), F32))
    return pl.pallas_call(
        functools.partial(_ret_mixer_kernel, seq=seq, sps=sps, latent=latent, has_s0=has_s0,
                          write_state=write_state),
        grid=(n_seq // sps, HEADS),
        in_specs=in_specs,
        out_specs=out_specs,
        out_shape=out_shape,
        scratch_shapes=[pltpu.VMEM((RC, RC), F32),
                        pltpu.VMEM((2, RC, LANES), F32),
                        pltpu.VMEM((2, RC, LANES), F32),
                        pltpu.VMEM((sps, seq // RC, DK, DV), BF16),
                        pltpu.VMEM((sps, DK, DV), F32),
                        pltpu.VMEM((rows, DK), BF16),
                        pltpu.VMEM((rows, DV), BF16)],
        compiler_params=_cparams(("parallel", "parallel")),
        name=f"ret_mixer_t{seq}",
    )(*args)


def _rope_tables():
    t = jnp.arange(DEC_SEQ)
    row = (t // GRID_W).astype(F32)
    col = (t % GRID_W).astype(F32)
    quarter = DK // 4
    inv = ROPE_BASE ** (-jnp.arange(quarter, dtype=F32) / quarter)
    ang_r = row[:, None] * inv
    ang_c = col[:, None] * inv
    cos_t = jnp.concatenate([jnp.cos(ang_r)] * 2 + [jnp.cos(ang_c)] * 2, axis=1)
    sin_t = jnp.concatenate([-jnp.sin(ang_r), jnp.sin(ang_r), -jnp.sin(ang_c), jnp.sin(ang_c)], axis=1)
    return cos_t, sin_t


def kernel(x_prompt, x_sample, state_ret, c, c_ctx, w_mod, b_mod, ln1_g, ln1_b, ln2_g, ln2_b,
           w_pw1, b_pw1, w_dw, b_dw, cn_g, cn_b, w_pw2, b_pw2,
           w_ret_in, ret_log2_rate, w_ret_o, w_ff1, b_ff1, w_ff2, b_ff2):
    xp = x_prompt.reshape(P_TOK, D)
    xs = x_sample.reshape(S_TOK, D)
    cond = jnp.concatenate(
        [c_ctx[None, :], c, jnp.zeros((N_COND - 1 - DEC_BATCH, D), F32)], axis=0)

    mod = _modulation(cond, w_mod, b_mod)
    mod = mod.reshape(DEPTH, N_COND, 6, D).transpose(0, 2, 1, 3).reshape(DEPTH * 6 * N_COND, 1, D)

    bf = lambda w: w.astype(BF16)
    row = lambda v: v.reshape(1, -1)

    x, hm = _conv_mixer(xp, xs, mod, bf(w_pw1[0]), row(b_pw1[0]), w_dw[0], row(b_dw[0]),
                        row(cn_g[0]), row(cn_b[0]), bf(w_pw2[0]), row(b_pw2[0]),
                        row(ln1_g[0]), row(ln1_b[0]))
    w1_tiles = _ffn_w1_tiles(w_ff1)
    w2_bf = bf(w_ff2)
    x, hm = _ffn(hm, w1_tiles, row(b_ff1[0]), w2_bf, row(b_ff2[0]),
                 x, mod, 0, row(ln2_g[0]), row(ln2_b[0]), last=False)

    cos_t, sin_t = _rope_tables()
    w_slabs = _ret_w_slabs(w_ret_in[0])
    rate = jnp.broadcast_to(ret_log2_rate[0][:, :, None, None], (2, HEADS, 1, DV))
    gp, new_state = _ret_mixer(rate, hm, w_slabs, None, None, seq=SEQ, n_seq=BATCH, sps=4,
                               row_block0=0, write_state=True)
    (gs,) = _ret_mixer(rate, hm, w_slabs, (cos_t, sin_t), state_ret, seq=DEC_SEQ,
                       n_seq=DEC_BATCH, sps=1, row_block0=P_TOK // DEC_SEQ, write_state=False)
    x, hm = _proj_res_ln(gp, gs, bf(w_ret_o[0]), x, mod, 1,
                         row(ln1_g[1]), row(ln1_b[1]), tm=256, sub=128)
    yp, ys = _ffn(hm, w1_tiles, row(b_ff1[1]), w2_bf, row(b_ff2[1]),
                  x, mod, 1, row(ln2_g[1]), row(ln2_b[1]), last=True)

    return (yp.reshape(BATCH, SEQ, D), ys.reshape(DEC_BATCH, DEC_SEQ, D), new_state)
```

```python
import functools

import jax
import jax.numpy as jnp
from jax import lax
from jax.experimental import pallas as pl
from jax.experimental.pallas import tpu as pltpu

F32 = jnp.float32
BF16 = jnp.bfloat16

D = 2048
BATCH = 16
SEQ = 256
DEC_BATCH = 4
DEC_SEQ = 2048
GRID_W = 64
CONV_WIDTH = 31
D_FF = 4 * D
HEADS = 8
DK = D // HEADS
DV = 2 * D // HEADS
ROPE_BASE = 10000.0
LN_EPS = 1e-5
GN_EPS = 1e-5
DEPTH = 2
ALPHA = (2.0 * DEPTH) ** 0.25

P_TOK = BATCH * SEQ
S_TOK = DEC_BATCH * DEC_SEQ
N_TOK = P_TOK + S_TOK
N_COND = 8

LANES = 128
SUBLANES = 8
TM = 512
VMEM_LIMIT = 60 * 1024 * 1024


def _cparams(sem):
    return pltpu.CompilerParams(dimension_semantics=sem, vmem_limit_bytes=VMEM_LIMIT)


def _cond_index(i, tm):
    r = i * tm
    return jnp.where(r < P_TOK, 0, 1 + (r - P_TOK) // DEC_SEQ)


def _mod_spec(layer, part, tm, row_axis=0):
    base = (layer * 6 + part) * N_COND
    return pl.BlockSpec((None, 1, D),
                        lambda *ids: (base + _cond_index(ids[row_axis], tm), 0, 0))


def _const_spec(shape):
    return pl.BlockSpec(shape, lambda *ids: (0,) * len(shape))


def _group_specs(tm, cols, row_axis=0):
    n_p = P_TOK // tm
    ctx = pl.BlockSpec((tm, cols), lambda *ids: (jnp.minimum(ids[row_axis], n_p - 1), 0))
    lat = pl.BlockSpec((tm, cols), lambda *ids: (jnp.maximum(ids[row_axis] - n_p, 0), 0))
    return [ctx, lat]


FFN_TF = 1024


def _ffn_cast_specs(layer, n_units, step_of):
    rows1 = (D_FF // FFN_TF) * D // n_units
    rows2 = D_FF // n_units
    per_tile = D // rows1
    unit = lambda ids: jnp.minimum(step_of(*ids), n_units - 1)
    in_specs = [
        pl.BlockSpec((None, rows1, FFN_TF),
                     lambda *ids: (layer, unit(ids) % per_tile, unit(ids) // per_tile)),
        pl.BlockSpec((None, rows2, D), lambda *ids: (layer, unit(ids), 0)),
    ]
    out_specs = [
        pl.BlockSpec((None, rows1, FFN_TF),
                     lambda *ids: (unit(ids) // per_tile, unit(ids) % per_tile, 0)),
        pl.BlockSpec((rows2, D), lambda *ids: (unit(ids), 0)),
    ]
    out_shape = [jax.ShapeDtypeStruct((D_FF // FFN_TF, D, FFN_TF), BF16),
                 jax.ShapeDtypeStruct((D_FF, D), BF16)]
    return in_specs, out_specs, out_shape


def _layer_norm(r, g, b):
    mu = jnp.mean(r, axis=-1, keepdims=True)
    d = r - mu
    var = jnp.mean(d * d, axis=-1, keepdims=True)
    return d * lax.rsqrt(var + LN_EPS) * g + b


def _silu(x):
    return x * jax.nn.sigmoid(x)


def _dot(a, b):
    return jnp.dot(a, b, preferred_element_type=F32)


def _mod_kernel(cond_ref, w_ref, b_ref, o_ref):
    s = _silu(cond_ref[...]).astype(BF16)
    o_ref[...] = _dot(s, w_ref[...].astype(BF16)) + b_ref[...]


def _modulation(cond, w_mod, b_mod):
    tn = 1024
    return pl.pallas_call(
        _mod_kernel,
        grid=(DEPTH, 6 * D // tn),
        in_specs=[
            pl.BlockSpec((N_COND, D), lambda l, j: (0, 0)),
            pl.BlockSpec((None, D, tn), lambda l, j: (l, 0, j)),
            pl.BlockSpec((None, 1, tn), lambda l, j: (l, 0, j)),
        ],
        out_specs=pl.BlockSpec((None, N_COND, tn), lambda l, j: (l, 0, j)),
        out_shape=jax.ShapeDtypeStruct((DEPTH, N_COND, 6 * D), F32),
        compiler_params=_cparams(("parallel", "parallel")),
        name="modulation",
    )(cond, w_mod, b_mod.reshape(DEPTH, 1, 6 * D))


CONV_TB = 256
CONV_HALO = 16
CONV_WIN = CONV_TB + 2 * CONV_HALO
CONV_CW = 256
CONV_RB = 32
CONV_SUB = 128


def _dw_taps(win_ref, sh_ref, wb_ref, y_ref, c0):
    shift = CONV_HALO - CONV_WIDTH // 2
    cols = slice(c0, c0 + CONV_CW)
    w0 = win_ref[:, cols]
    for s in range(1, SUBLANES):
        sh_ref[s - 1] = pltpu.roll(w0, CONV_WIN - s, axis=0)
    groups = CONV_RB // SUBLANES
    for r in range(0, CONV_TB, CONV_RB):
        accs = [jnp.zeros((SUBLANES, CONV_CW), F32)] * groups
        for k in range(CONV_WIDTH):
            off = shift + k
            s, base = off % SUBLANES, r + off - off % SUBLANES
            if s == 0:
                src = win_ref[base:base + CONV_RB, cols]
            else:
                src = sh_ref[s - 1, base:base + CONV_RB, :]
            wk = wb_ref[k, :, cols]
            accs = [a + src[g * SUBLANES:(g + 1) * SUBLANES] * wk for g, a in enumerate(accs)]
        y_ref[r:r + CONV_RB, cols] = jnp.concatenate(accs, axis=0)


def _conv_mixer_kernel(xp_ref, xs_ref, prev_ref, next_ref, sc1_ref, sh1_ref, gate_ref,
                       sc2_ref, sh2_ref, w1_ref, b1_ref, wdw_ref, bdw_ref, cng_ref, cnb_ref,
                       w2_ref, b2_ref, lng_ref, lnb_ref, f1_ref, f2_ref,
                       xo_ref, hmo_ref, f1o_ref, f2o_ref,
                       hm_ref, win_ref, sh_ref, wb_ref, y_ref):
    i = pl.program_id(0)
    f1o_ref[...] = f1_ref[...].astype(BF16)
    f2o_ref[...] = f2_ref[...].astype(BF16)

    @pl.when(i == 0)
    def _():
        for k in range(CONV_WIDTH):
            wb_ref[k] = jnp.broadcast_to(wdw_ref[pl.ds(k, 1), :], (SUBLANES, D))

    n_p = P_TOK // CONV_TB
    per_seq = DEC_SEQ // CONV_TB
    pos = (i - n_p) % per_seq
    latent = i >= n_p
    has_prev = jnp.logical_and(latent, pos != 0)
    has_next = jnp.logical_and(latent, pos != per_seq - 1)
    lo, hi = CONV_HALO, CONV_HALO + CONV_TB

    def modulate(x):
        return (x * (1.0 + sc1_ref[...]) + sh1_ref[...]).astype(BF16)

    hm_ref[0:lo, :] = modulate(prev_ref[...])
    hm_ref[lo:hi, :] = modulate(jnp.where(latent, xs_ref[...], xp_ref[...]))
    hm_ref[hi:, :] = modulate(next_ref[...])

    for c0 in range(0, D, CONV_CW):
        hm = hm_ref[...]
        a = _dot(hm, w1_ref[:, c0:c0 + CONV_CW]) + b1_ref[:, c0:c0 + CONV_CW]
        g = _dot(hm, w1_ref[:, D + c0:D + c0 + CONV_CW]) + b1_ref[:, D + c0:D + c0 + CONV_CW]
        u = a * jax.nn.sigmoid(g)
        win_ref[0:lo, c0:c0 + CONV_CW] = jnp.where(has_prev, u[0:lo], 0.0)
        win_ref[lo:hi, c0:c0 + CONV_CW] = u[lo:hi]
        win_ref[hi:, c0:c0 + CONV_CW] = jnp.where(has_next, u[hi:], 0.0)
        _dw_taps(win_ref, sh_ref, wb_ref, y_ref, c0)

    v = _silu(_layer_norm(y_ref[...] + bdw_ref[...], cng_ref[...], cnb_ref[...])).astype(BF16)
    for s in range(CONV_TB // CONV_SUB):
        rows = slice(s * CONV_SUB, (s + 1) * CONV_SUB)
        y2 = _dot(v[rows], w2_ref[...]) + b2_ref[...]
        x = jnp.where(latent, xs_ref[rows, :], xp_ref[rows, :])
        xn = _layer_norm(ALPHA * x + gate_ref[...] * y2, lng_ref[...], lnb_ref[...])
        xo_ref[rows, :] = xn
        hmo_ref[rows, :] = (xn * (1.0 + sc2_ref[...]) + sh2_ref[...]).astype(BF16)


def _conv_mixer(xp, xs, mod, w_pw1, b_pw1, w_dw, b_dw, cn_g, cn_b, w_pw2, b_pw2, ln_g, ln_b,
                w_ff1, w_ff2):
    tb = CONV_TB
    n_p = P_TOK // tb
    hb = tb // CONV_HALO
    last_halo = S_TOK // CONV_HALO - 1
    vec = _const_spec((1, D))
    resident = lambda shape: pl.BlockSpec(shape, lambda i: (0, 0), pipeline_mode=pl.Buffered(1))
    cast_in, cast_out, cast_shape = _ffn_cast_specs(0, 32, lambda i: i)
    return pl.pallas_call(
        _conv_mixer_kernel,
        grid=(N_TOK // tb,),
        in_specs=_group_specs(tb, D) + [
            pl.BlockSpec((CONV_HALO, D), lambda i: (jnp.clip((i - n_p) * hb - 1, 0, last_halo), 0)),
            pl.BlockSpec((CONV_HALO, D), lambda i: (jnp.clip((i - n_p + 1) * hb, 0, last_halo), 0)),
            _mod_spec(0, 1, tb), _mod_spec(0, 0, tb), _mod_spec(0, 2, tb),
            _mod_spec(0, 4, tb), _mod_spec(0, 3, tb),
            resident((D, 2 * D)), _const_spec((1, 2 * D)),
            _const_spec((CONV_WIDTH, D)), vec, vec, vec,
            resident((D, D)), vec, vec, vec,
        ] + cast_in,
        out_specs=[pl.BlockSpec((tb, D), lambda i: (i, 0)),
                   pl.BlockSpec((tb, D), lambda i: (i, 0))] + cast_out,
        out_shape=[jax.ShapeDtypeStruct((N_TOK, D), F32),
                   jax.ShapeDtypeStruct((N_TOK, D), BF16)] + cast_shape,
        scratch_shapes=[pltpu.VMEM((CONV_WIN, D), BF16),
                        pltpu.VMEM((CONV_WIN, D), F32),
                        pltpu.VMEM((SUBLANES - 1, CONV_WIN, CONV_CW), F32),
                        pltpu.VMEM((CONV_WIDTH, SUBLANES, D), F32),
                        pltpu.VMEM((CONV_TB, D), F32)],
        compiler_params=_cparams(("arbitrary",)),
        name="conv_mixer",
    )(xp, xs, xs, xs, mod, mod, mod, mod, mod, w_pw1, b_pw1, w_dw, b_dw, cn_g, cn_b,
      w_pw2, b_pw2, ln_g, ln_b, w_ff1, w_ff2)


EPI_SUB = 256


def _proj_res_ln_kernel(ap_ref, as_ref, w_ref, x_ref, gate_ref, lng_ref, lnb_ref, sc_ref, sh_ref,
                        xo_ref, hmo_ref, *, n_p, tm, sub):
    i = pl.program_id(0)
    for s in range(tm // sub):
        rows = pl.ds(s * sub, sub)
        y = _dot(jnp.where(i < n_p, ap_ref[rows, :], as_ref[rows, :]), w_ref[...])
        xn = _layer_norm(ALPHA * x_ref[rows, :] + gate_ref[...] * y, lng_ref[...], lnb_ref[...])
        xo_ref[rows, :] = xn
        hmo_ref[rows, :] = (xn * (1.0 + sc_ref[...]) + sh_ref[...]).astype(BF16)


def _proj_res_ln(a_ctx, a_lat, w, x, mod, layer, ln_g, ln_b, tm, sub):
    kdim = w.shape[0]
    return pl.pallas_call(
        functools.partial(_proj_res_ln_kernel, n_p=P_TOK // tm, tm=tm, sub=sub),
        grid=(N_TOK // tm,),
        in_specs=_group_specs(tm, kdim) + [
            pl.BlockSpec((kdim, D), lambda i: (0, 0), pipeline_mode=pl.Buffered(1)),
            pl.BlockSpec((tm, D), lambda i: (i, 0)),
            _mod_spec(layer, 2, tm), _const_spec((1, D)), _const_spec((1, D)),
            _mod_spec(layer, 4, tm), _mod_spec(layer, 3, tm)],
        out_specs=[pl.BlockSpec((tm, D), lambda i: (i, 0)),
                   pl.BlockSpec((tm, D), lambda i: (i, 0))],
        out_shape=[jax.ShapeDtypeStruct((N_TOK, D), F32),
                   jax.ShapeDtypeStruct((N_TOK, D), BF16)],
        compiler_params=_cparams(("parallel",)),
        name=f"proj_res_ln_l{layer}",
    )(a_ctx, a_lat, w, x, mod, ln_g, ln_b, mod, mod)


def _ffn_kernel(*refs, nf, n_p, last):
    hm_ref, w1_ref, b1_ref, w2_ref, b2_ref, x_ref, gate_ref, lng_ref, lnb_ref = refs[:9]
    if last:
        yp_ref, ys_ref, acc_ref = refs[9:]
    else:
        sc_ref, sh_ref, f1_ref, f2_ref, xo_ref, hmo_ref, f1o_ref, f2o_ref, acc_ref = refs[9:]
        f1o_ref[...] = f1_ref[...].astype(BF16)
        f2o_ref[...] = f2_ref[...].astype(BF16)
    i = pl.program_id(0)
    f = pl.program_id(1)
    tm = hm_ref.shape[0]

    h = _dot(hm_ref[...], w1_ref[...]) + b1_ref[...]
    h = jnp.square(jnp.maximum(h, 0.0)).astype(BF16)

    @pl.when(f == 0)
    def _():
        acc_ref[...] = _dot(h, w2_ref[...])

    @pl.when(jnp.logical_and(f > 0, f < nf - 1))
    def _():
        acc_ref[...] += _dot(h, w2_ref[...])

    def finish(store):
        for s in range(tm // EPI_SUB):
            rows = pl.ds(s * EPI_SUB, EPI_SUB)
            y = (acc_ref[rows, :] + _dot(h[s * EPI_SUB:(s + 1) * EPI_SUB], w2_ref[...])
                 + b2_ref[...])
            store(rows, _layer_norm(ALPHA * x_ref[rows, :] + gate_ref[...] * y,
                                    lng_ref[...], lnb_ref[...]))

    def store_ctx(rows, xn):
        yp_ref[rows, :] = xn

    def store_lat(rows, xn):
        ys_ref[rows, :] = xn

    def store_next(rows, xn):
        xo_ref[rows, :] = xn
        hmo_ref[rows, :] = (xn * (1.0 + sc_ref[...]) + sh_ref[...]).astype(BF16)

    is_last_f = f == nf - 1
    if last:
        pl.when(jnp.logical_and(is_last_f, i < n_p))(lambda: finish(store_ctx))
        pl.when(jnp.logical_and(is_last_f, i >= n_p))(lambda: finish(store_lat))
    else:
        pl.when(is_last_f)(lambda: finish(store_next))


def _ffn(hm, w1_tiles, b1, w2, b2, x, mod, layer, ln_g, ln_b, next_w=None):
    tm, tf = TM, FFN_TF
    nf = D_FF // tf
    n_p = P_TOK // tm
    last = next_w is None
    in_specs = [
        pl.BlockSpec((tm, D), lambda i, f: (i, 0)),
        pl.BlockSpec((None, D, tf), lambda i, f: (f, 0, 0)),
        pl.BlockSpec((1, tf), lambda i, f: (0, f)),
        pl.BlockSpec((tf, D), lambda i, f: (f, 0)),
        _const_spec((1, D)),
        pl.BlockSpec((tm, D), lambda i, f: (i, 0)),
        _mod_spec(layer, 5, tm),
        _const_spec((1, D)), _const_spec((1, D)),
    ]
    args = [hm, w1_tiles, b1, w2, b2, x, mod, ln_g, ln_b]
    if last:
        out_specs = _group_specs(tm, D)
        out_shape = [jax.ShapeDtypeStruct((P_TOK, D), F32), jax.ShapeDtypeStruct((S_TOK, D), F32)]
    else:
        cast_in, cast_out, cast_shape = _ffn_cast_specs(layer + 1, 128, lambda i, f: i * nf + f)
        in_specs += [_mod_spec(layer + 1, 1, tm), _mod_spec(layer + 1, 0, tm)] + cast_in
        args += [mod, mod, *next_w]
        out_specs = [pl.BlockSpec((tm, D), lambda i, f: (i, 0))] * 2 + cast_out
        out_shape = [jax.ShapeDtypeStruct((N_TOK, D), F32),
                     jax.ShapeDtypeStruct((N_TOK, D), BF16)] + cast_shape
    return pl.pallas_call(
        functools.partial(_ffn_kernel, nf=nf, n_p=n_p, last=last),
        grid=(N_TOK // tm, nf),
        in_specs=in_specs,
        out_specs=out_specs,
        out_shape=out_shape,
        scratch_shapes=[pltpu.VMEM((tm, D), F32)],
        compiler_params=_cparams(("arbitrary", "arbitrary")),
        name=f"ffn_l{layer}",
    )(*args)


RC = 256
RET_SLAB_TN = 1024


def _scale_rows(a, d):
    return jnp.concatenate(
        [a[:, s * LANES:(s + 1) * LANES] * d for s in range(a.shape[1] // LANES)], axis=1)


def _ret_w_slabs_kernel(w_ref, o_ref):
    for s in range(RET_SLAB_TN // DK):
        o_ref[s] = w_ref[:, s * DK:(s + 1) * DK].astype(BF16)


def _ret_w_slabs(w_in):
    return pl.pallas_call(
        _ret_w_slabs_kernel,
        grid=(6 * D // RET_SLAB_TN,),
        in_specs=[pl.BlockSpec((D, RET_SLAB_TN), lambda j: (0, j))],
        out_specs=pl.BlockSpec((RET_SLAB_TN // DK, D, DK), lambda j: (j, 0, 0)),
        out_shape=jax.ShapeDtypeStruct((6 * D // DK, D, DK), BF16),
        compiler_params=_cparams(("parallel",)),
        name="ret_w_slabs",
    )(w_in)


def _rope_rows(y, cos, sin):
    half = DK // 2
    outs = []
    for s in range(2):
        x = y[:, s * half:(s + 1) * half]
        rot = pltpu.roll(x, half // 2, axis=1)
        outs.append(x * cos[:, s * half:(s + 1) * half] + rot * sin[:, s * half:(s + 1) * half])
    return jnp.concatenate(outs, axis=1)


def _ret_mixer_kernel(*refs, seq, sps, latent, has_s0, write_state):
    refs = list(refs)
    rate_ref, hm_ref, wq_ref, wk_ref, wv_ref, wg_ref = [refs.pop(0) for _ in range(6)]
    cos_ref, sin_ref = (refs.pop(0), refs.pop(0)) if latent else (None, None)
    s0_ref = refs.pop(0) if has_s0 else None
    o_ref = refs.pop(0)
    so_ref = refs.pop(0) if write_state else None
    mask_ref, din_ref, dout_ref, sb_all, s_all, k_s, v_s = refs
    n = seq // RC
    c_f = float(RC)

    lg = jnp.log1p(-jnp.exp2(rate_ref[...]))
    lgf, lgb = lg[0], lg[1]
    dchunk_f = jnp.exp(lgf * c_f)
    dchunk_b = jnp.exp(lgb * c_f)
    rows_f = lax.broadcasted_iota(jnp.int32, (RC, LANES), 0).astype(F32)
    diff = (lax.broadcasted_iota(jnp.int32, (RC, RC), 0)
            - lax.broadcasted_iota(jnp.int32, (RC, RC), 1)).astype(F32)
    mf = jnp.where(diff >= 0, jnp.exp(lgf[:, :RC] * jnp.maximum(diff, 0.0)), 0.0)
    mb = jnp.where(diff <= 0, jnp.exp(lgb[:, :RC] * jnp.maximum(-diff, 0.0)), 0.0)
    mask_ref[...] = mf + mb
    lf, lb = lgf[:, :LANES], lgb[:, :LANES]
    din_ref[0] = jnp.exp(lf * (rows_f + 1.0))
    din_ref[1] = jnp.exp(lb * (c_f - rows_f))
    dout_ref[0] = jnp.exp(lf * (c_f - 1.0 - rows_f))
    dout_ref[1] = jnp.exp(lb * rows_f)

    def loop(body, unroll):
        if n <= 2:
            for t in range(n):
                body(t)
        else:
            lax.fori_loop(0, n, lambda t, carry: (body(t), carry)[1], 0, unroll=unroll)

    def rope(y, c):
        if not latent:
            return y
        pos = pl.ds(c * RC, RC) if isinstance(c, int) else pl.ds(pl.multiple_of(c * RC, RC), RC)
        return _rope_rows(y, cos_ref[pos, :], sin_ref[pos, :])

    def two(w_ref, x):
        return jnp.concatenate([_dot(x, w_ref[0]), _dot(x, w_ref[1])], axis=1)

    for sq in range(sps):
        sb_ref, s_ref = sb_all.at[sq], s_all.at[sq]

        def chunk_rows(c, sq=sq):
            if isinstance(c, int):
                return pl.ds(sq * seq + c * RC, RC)
            return pl.ds(pl.multiple_of(sq * seq + c * RC, RC), RC)

        def init_state(direction, sq=sq, s_ref=s_ref):
            if has_s0:
                s_ref[...] = s0_ref[sq, direction]
            else:
                s_ref[...] = jnp.zeros((DK, DV), F32)

        def state_update(c, direction, dchunk, chunk_rows=chunk_rows, s_ref=s_ref):
            rows = chunk_rows(c)
            kd = _scale_rows(k_s[rows, :].astype(F32), dout_ref[direction]).astype(BF16)
            upd = lax.dot_general(kd, v_s[rows, :], (((0,), (0,)), ((), ())),
                                  preferred_element_type=F32)
            s_ref[...] = dchunk * s_ref[...] + upd

        def phase0(c, chunk_rows=chunk_rows):
            rows = chunk_rows(c)
            x = hm_ref[rows, :]
            k_s[rows, :] = rope(_dot(x, wk_ref[...]), c).astype(BF16)
            v_s[rows, :] = two(wv_ref, x).astype(BF16)

        loop(phase0, 2)

        init_state(1)

        def phase1(t, sb_ref=sb_ref, s_ref=s_ref, state_update=state_update):
            c = n - 1 - t
            sb_ref[c] = s_ref[...].astype(BF16)
            state_update(c, 1, dchunk_b)

        loop(phase1, 4)
        if write_state:
            so_ref[sq, 1] = s_ref[...]

        init_state(0)

        def phase2(c, sb_ref=sb_ref, s_ref=s_ref, chunk_rows=chunk_rows,
                   state_update=state_update):
            rows = chunk_rows(c)
            x = hm_ref[rows, :]
            qc = rope(_dot(x, wq_ref[...]) * DK ** -0.5, c).astype(BF16)
            gate = _silu(two(wg_ref, x))
            vc = v_s[rows, :]
            scores = lax.dot_general(qc, k_s[rows, :], (((1,), (1,)), ((), ())),
                                     preferred_element_type=F32)
            tot = _dot((scores * mask_ref[...]).astype(BF16), vc)
            if has_s0 or not (isinstance(c, int) and c == 0):
                tot = tot + _scale_rows(_dot(qc, s_ref[...].astype(BF16)), din_ref[0])
            if has_s0 or not (isinstance(c, int) and c == n - 1):
                tot = tot + _scale_rows(_dot(qc, sb_ref[c]), din_ref[1])
            mu = jnp.mean(tot, axis=-1, keepdims=True)
            d = tot - mu
            var = jnp.mean(d * d, axis=-1, keepdims=True)
            nrm = d * lax.rsqrt(var + GN_EPS)
            o_ref[rows, :] = (gate * nrm).astype(BF16)
            state_update(c, 0, dchunk_f)

        loop(phase2, 2)
        if write_state:
            so_ref[sq, 0] = s_ref[...]


def _ret_mixer(rate, hm, w_slabs, tables, s0, *, seq, n_seq, sps, row_block0, write_state):
    has_s0 = s0 is not None
    latent = tables is not None
    koff, voff, goff = D // DK, 2 * D // DV, 4 * D // DV
    rows = sps * seq
    state_block = (sps, None, 2, None, DK, DV)
    in_specs = [
        pl.BlockSpec((2, None, 1, DV), lambda b, h: (0, h, 0, 0)),
        pl.BlockSpec((rows, D), lambda b, h: (row_block0 + b, 0)),
        pl.BlockSpec((None, D, DK), lambda b, h: (h, 0, 0)),
        pl.BlockSpec((None, D, DK), lambda b, h: (koff + h, 0, 0)),
        pl.BlockSpec((2, D, DK), lambda b, h: (voff + h, 0, 0)),
        pl.BlockSpec((2, D, DK), lambda b, h: (goff + h, 0, 0)),
    ]
    args = [rate, hm, w_slabs, w_slabs, w_slabs, w_slabs]
    if latent:
        in_specs += [_const_spec((DEC_SEQ, DK)), _const_spec((DEC_SEQ, DK))]
        args += list(tables)
    if has_s0:
        in_specs.append(pl.BlockSpec(state_block, lambda b, h: (b, 0, 0, h, 0, 0)))
        args.append(s0)
    out_specs = [pl.BlockSpec((rows, DV), lambda b, h: (b, h))]
    out_shape = [jax.ShapeDtypeStruct((n_seq * seq, 2 * D), BF16)]
    if write_state:
        out_specs.append(pl.BlockSpec(state_block, lambda b, h: (b, 0, 0, h, 0, 0)))
        out_shape.append(jax.ShapeDtypeStruct((n_seq, 1, 2, HEADS, DK, DV), F32))
    return pl.pallas_call(
        functools.partial(_ret_mixer_kernel, seq=seq, sps=sps, latent=latent, has_s0=has_s0,
                          write_state=write_state),
        grid=(n_seq // sps, HEADS),
        in_specs=in_specs,
        out_specs=out_specs,
        out_shape=out_shape,
        scratch_shapes=[pltpu.VMEM((RC, RC), F32),
                        pltpu.VMEM((2, RC, LANES), F32),
                        pltpu.VMEM((2, RC, LANES), F32),
                        pltpu.VMEM((sps, seq // RC, DK, DV), BF16),
                        pltpu.VMEM((sps, DK, DV), F32),
                        pltpu.VMEM((rows, DK), BF16),
                        pltpu.VMEM((rows, DV), BF16)],
        compiler_params=_cparams(("parallel", "parallel")),
        name=f"ret_mixer_t{seq}",
    )(*args)


def _rope_tables():
    t = jnp.arange(DEC_SEQ)
    row = (t // GRID_W).astype(F32)
    col = (t % GRID_W).astype(F32)
    quarter = DK // 4
    inv = ROPE_BASE ** (-jnp.arange(quarter, dtype=F32) / quarter)
    ang_r = row[:, None] * inv
    ang_c = col[:, None] * inv
    cos_t = jnp.concatenate([jnp.cos(ang_r)] * 2 + [jnp.cos(ang_c)] * 2, axis=1)
    sin_t = jnp.concatenate([-jnp.sin(ang_r), jnp.sin(ang_r), -jnp.sin(ang_c), jnp.sin(ang_c)], axis=1)
    return cos_t, sin_t


def kernel(x_prompt, x_sample, state_ret, c, c_ctx, w_mod, b_mod, ln1_g, ln1_b, ln2_g, ln2_b,
           w_pw1, b_pw1, w_dw, b_dw, cn_g, cn_b, w_pw2, b_pw2,
           w_ret_in, ret_log2_rate, w_ret_o, w_ff1, b_ff1, w_ff2, b_ff2):
    xp = x_prompt.reshape(P_TOK, D)
    xs = x_sample.reshape(S_TOK, D)
    cond = jnp.concatenate(
        [c_ctx[None, :], c, jnp.zeros((N_COND - 1 - DEC_BATCH, D), F32)], axis=0)

    mod = _modulation(cond, w_mod, b_mod)
    mod = mod.reshape(DEPTH, N_COND, 6, D).transpose(0, 2, 1, 3).reshape(DEPTH * 6 * N_COND, 1, D)

    bf = lambda w: w.astype(BF16)
    row = lambda v: v.reshape(1, -1)

    x, hm, w1_l0, w2_l0 = _conv_mixer(
        xp, xs, mod, bf(w_pw1[0]), row(b_pw1[0]), w_dw[0], row(b_dw[0]), row(cn_g[0]),
        row(cn_b[0]), bf(w_pw2[0]), row(b_pw2[0]), row(ln1_g[0]), row(ln1_b[0]), w_ff1, w_ff2)
    x, hm, w1_l1, w2_l1 = _ffn(hm, w1_l0, row(b_ff1[0]), w2_l0, row(b_ff2[0]), x, mod, 0,
                               row(ln2_g[0]), row(ln2_b[0]), next_w=(w_ff1, w_ff2))

    cos_t, sin_t = _rope_tables()
    w_slabs = _ret_w_slabs(w_ret_in[0])
    rate = jnp.broadcast_to(ret_log2_rate[0][:, :, None, None], (2, HEADS, 1, DV))
    gp, new_state = _ret_mixer(rate, hm, w_slabs, None, None, seq=SEQ, n_seq=BATCH, sps=4,
                               row_block0=0, write_state=True)
    (gs,) = _ret_mixer(rate, hm, w_slabs, (cos_t, sin_t), state_ret, seq=DEC_SEQ,
                       n_seq=DEC_BATCH, sps=1, row_block0=P_TOK // DEC_SEQ, write_state=False)
    x, hm = _proj_res_ln(gp, gs, bf(w_ret_o[0]), x, mod, 1,
                         row(ln1_g[1]), row(ln1_b[1]), tm=256, sub=128)
    yp, ys = _ffn(hm, w1_l1, row(b_ff1[1]), w2_l1, row(b_ff2[1]),
                  x, mod, 1, row(ln2_g[1]), row(ln2_b[1]))

    return (yp.reshape(BATCH, SEQ, D), ys.reshape(DEC_BATCH, DEC_SEQ, D), new_state)
```

```python
import functools

import jax
import jax.numpy as jnp
from jax import lax
from jax.experimental import pallas as pl
from jax.experimental.pallas import tpu as pltpu

F32 = jnp.float32
BF16 = jnp.bfloat16

D = 2048
BATCH = 16
SEQ = 256
DEC_BATCH = 4
DEC_SEQ = 2048
GRID_W = 64
CONV_WIDTH = 31
D_FF = 4 * D
HEADS = 8
DK = D // HEADS
DV = 2 * D // HEADS
ROPE_BASE = 10000.0
LN_EPS = 1e-5
GN_EPS = 1e-5
DEPTH = 2
ALPHA = (2.0 * DEPTH) ** 0.25

P_TOK = BATCH * SEQ
S_TOK = DEC_BATCH * DEC_SEQ
N_TOK = P_TOK + S_TOK
N_COND = 8

LANES = 128
SUBLANES = 8
TM = 512
VMEM_LIMIT = 60 * 1024 * 1024


def _cparams(sem):
    return pltpu.CompilerParams(dimension_semantics=sem, vmem_limit_bytes=VMEM_LIMIT)


def _cond_index(i, tm):
    r = i * tm
    return jnp.where(r < P_TOK, 0, 1 + (r - P_TOK) // DEC_SEQ)


def _mod_spec(layer, part, tm, row_axis=0):
    base = (layer * 6 + part) * N_COND
    return pl.BlockSpec((None, 1, D),
                        lambda *ids: (base + _cond_index(ids[row_axis], tm), 0, 0))


def _const_spec(shape):
    return pl.BlockSpec(shape, lambda *ids: (0,) * len(shape))


def _group_specs(tm, cols, row_axis=0):
    n_p = P_TOK // tm
    ctx = pl.BlockSpec((tm, cols), lambda *ids: (jnp.minimum(ids[row_axis], n_p - 1), 0))
    lat = pl.BlockSpec((tm, cols), lambda *ids: (jnp.maximum(ids[row_axis] - n_p, 0), 0))
    return [ctx, lat]


FFN_TF = 1024


def _ffn_cast_specs(layer, n_units, step_of):
    rows1 = (D_FF // FFN_TF) * D // n_units
    rows2 = D_FF // n_units
    per_tile = D // rows1
    unit = lambda ids: jnp.minimum(step_of(*ids), n_units - 1)
    in_specs = [
        pl.BlockSpec((None, rows1, FFN_TF),
                     lambda *ids: (layer, unit(ids) % per_tile, unit(ids) // per_tile)),
        pl.BlockSpec((None, rows2, D), lambda *ids: (layer, unit(ids), 0)),
    ]
    out_specs = [
        pl.BlockSpec((None, rows1, FFN_TF),
                     lambda *ids: (unit(ids) // per_tile, unit(ids) % per_tile, 0)),
        pl.BlockSpec((rows2, D), lambda *ids: (unit(ids), 0)),
    ]
    out_shape = [jax.ShapeDtypeStruct((D_FF // FFN_TF, D, FFN_TF), BF16),
                 jax.ShapeDtypeStruct((D_FF, D), BF16)]
    return in_specs, out_specs, out_shape


def _layer_norm(r, g, b):
    mu = jnp.mean(r, axis=-1, keepdims=True)
    d = r - mu
    var = jnp.mean(d * d, axis=-1, keepdims=True)
    return d * lax.rsqrt(var + LN_EPS) * g + b


def _silu(x):
    return x * jax.nn.sigmoid(x)


def _dot(a, b):
    return jnp.dot(a, b, preferred_element_type=F32)


def _mod_kernel(cond_ref, w_ref, b_ref, o_ref):
    s = _silu(cond_ref[...]).astype(BF16)
    o_ref[...] = _dot(s, w_ref[...].astype(BF16)) + b_ref[...]


def _modulation(cond, w_mod, b_mod):
    tn = 1024
    return pl.pallas_call(
        _mod_kernel,
        grid=(DEPTH, 6 * D // tn),
        in_specs=[
            pl.BlockSpec((N_COND, D), lambda l, j: (0, 0)),
            pl.BlockSpec((None, D, tn), lambda l, j: (l, 0, j)),
            pl.BlockSpec((None, 1, tn), lambda l, j: (l, 0, j)),
        ],
        out_specs=pl.BlockSpec((None, N_COND, tn), lambda l, j: (l, 0, j)),
        out_shape=jax.ShapeDtypeStruct((DEPTH, N_COND, 6 * D), F32),
        compiler_params=_cparams(("parallel", "parallel")),
        name="modulation",
    )(cond, w_mod, b_mod.reshape(DEPTH, 1, 6 * D))


CONV_TB = 256
CONV_HALO = 16
CONV_WIN = CONV_TB + 2 * CONV_HALO
CONV_CW = 256
CONV_RB = 32
CONV_SUB = 128


def _dw_taps(win_ref, sh_ref, wb_ref, y_ref, c0):
    shift = CONV_HALO - CONV_WIDTH // 2
    cols = slice(c0, c0 + CONV_CW)
    w0 = win_ref[:, cols]
    for s in range(1, SUBLANES):
        sh_ref[s - 1] = pltpu.roll(w0, CONV_WIN - s, axis=0)
    groups = CONV_RB // SUBLANES
    for r in range(0, CONV_TB, CONV_RB):
        accs = [jnp.zeros((SUBLANES, CONV_CW), F32)] * groups
        for k in range(CONV_WIDTH):
            off = shift + k
            s, base = off % SUBLANES, r + off - off % SUBLANES
            if s == 0:
                src = win_ref[base:base + CONV_RB, cols]
            else:
                src = sh_ref[s - 1, base:base + CONV_RB, :]
            wk = wb_ref[k, :, cols]
            accs = [a + src[g * SUBLANES:(g + 1) * SUBLANES] * wk for g, a in enumerate(accs)]
        y_ref[r:r + CONV_RB, cols] = jnp.concatenate(accs, axis=0)


def _conv_mixer_kernel(xp_ref, xs_ref, prev_ref, next_ref, sc1_ref, sh1_ref, gate_ref,
                       sc2_ref, sh2_ref, w1_ref, b1_ref, wdw_ref, bdw_ref, cng_ref, cnb_ref,
                       w2_ref, b2_ref, lng_ref, lnb_ref, f1_ref, f2_ref,
                       xo_ref, hmo_ref, f1o_ref, f2o_ref,
                       hm_ref, win_ref, sh_ref, wb_ref, y_ref):
    i = pl.program_id(0)
    f1o_ref[...] = f1_ref[...].astype(BF16)
    f2o_ref[...] = f2_ref[...].astype(BF16)

    @pl.when(i == 0)
    def _():
        for k in range(CONV_WIDTH):
            wb_ref[k] = jnp.broadcast_to(wdw_ref[pl.ds(k, 1), :], (SUBLANES, D))

    n_p = P_TOK // CONV_TB
    per_seq = DEC_SEQ // CONV_TB
    pos = (i - n_p) % per_seq
    latent = i >= n_p
    has_prev = jnp.logical_and(latent, pos != 0)
    has_next = jnp.logical_and(latent, pos != per_seq - 1)
    lo, hi = CONV_HALO, CONV_HALO + CONV_TB

    def modulate(x):
        return (x * (1.0 + sc1_ref[...]) + sh1_ref[...]).astype(BF16)

    hm_ref[0:lo, :] = modulate(prev_ref[...])
    hm_ref[lo:hi, :] = modulate(jnp.where(latent, xs_ref[...], xp_ref[...]))
    hm_ref[hi:, :] = modulate(next_ref[...])

    for c0 in range(0, D, CONV_CW):
        hm = hm_ref[...]
        a = _dot(hm, w1_ref[:, c0:c0 + CONV_CW]) + b1_ref[:, c0:c0 + CONV_CW]
        g = _dot(hm, w1_ref[:, D + c0:D + c0 + CONV_CW]) + b1_ref[:, D + c0:D + c0 + CONV_CW]
        u = a * jax.nn.sigmoid(g)
        win_ref[0:lo, c0:c0 + CONV_CW] = jnp.where(has_prev, u[0:lo], 0.0)
        win_ref[lo:hi, c0:c0 + CONV_CW] = u[lo:hi]
        win_ref[hi:, c0:c0 + CONV_CW] = jnp.where(has_next, u[hi:], 0.0)
        _dw_taps(win_ref, sh_ref, wb_ref, y_ref, c0)

    v = _silu(_layer_norm(y_ref[...] + bdw_ref[...], cng_ref[...], cnb_ref[...])).astype(BF16)
    for s in range(CONV_TB // CONV_SUB):
        rows = slice(s * CONV_SUB, (s + 1) * CONV_SUB)
        y2 = _dot(v[rows], w2_ref[...]) + b2_ref[...]
        x = jnp.where(latent, xs_ref[rows, :], xp_ref[rows, :])
        xn = _layer_norm(ALPHA * x + gate_ref[...] * y2, lng_ref[...], lnb_ref[...])
        xo_ref[rows, :] = xn
        hmo_ref[rows, :] = (xn * (1.0 + sc2_ref[...]) + sh2_ref[...]).astype(BF16)


def _conv_mixer(xp, xs, mod, w_pw1, b_pw1, w_dw, b_dw, cn_g, cn_b, w_pw2, b_pw2, ln_g, ln_b,
                w_ff1, w_ff2):
    tb = CONV_TB
    n_p = P_TOK // tb
    hb = tb // CONV_HALO
    last_halo = S_TOK // CONV_HALO - 1
    vec = _const_spec((1, D))
    resident = lambda shape: pl.BlockSpec(shape, lambda i: (0, 0), pipeline_mode=pl.Buffered(1))
    cast_in, cast_out, cast_shape = _ffn_cast_specs(0, 32, lambda i: i)
    return pl.pallas_call(
        _conv_mixer_kernel,
        grid=(N_TOK // tb,),
        in_specs=_group_specs(tb, D) + [
            pl.BlockSpec((CONV_HALO, D), lambda i: (jnp.clip((i - n_p) * hb - 1, 0, last_halo), 0)),
            pl.BlockSpec((CONV_HALO, D), lambda i: (jnp.clip((i - n_p + 1) * hb, 0, last_halo), 0)),
            _mod_spec(0, 1, tb), _mod_spec(0, 0, tb), _mod_spec(0, 2, tb),
            _mod_spec(0, 4, tb), _mod_spec(0, 3, tb),
            resident((D, 2 * D)), _const_spec((1, 2 * D)),
            _const_spec((CONV_WIDTH, D)), vec, vec, vec,
            resident((D, D)), vec, vec, vec,
        ] + cast_in,
        out_specs=[pl.BlockSpec((tb, D), lambda i: (i, 0)),
                   pl.BlockSpec((tb, D), lambda i: (i, 0))] + cast_out,
        out_shape=[jax.ShapeDtypeStruct((N_TOK, D), F32),
                   jax.ShapeDtypeStruct((N_TOK, D), BF16)] + cast_shape,
        scratch_shapes=[pltpu.VMEM((CONV_WIN, D), BF16),
                        pltpu.VMEM((CONV_WIN, D), F32),
                        pltpu.VMEM((SUBLANES - 1, CONV_WIN, CONV_CW), F32),
                        pltpu.VMEM((CONV_WIDTH, SUBLANES, D), F32),
                        pltpu.VMEM((CONV_TB, D), F32)],
        compiler_params=_cparams(("arbitrary",)),
        name="conv_mixer",
    )(xp, xs, xs, xs, mod, mod, mod, mod, mod, w_pw1, b_pw1, w_dw, b_dw, cn_g, cn_b,
      w_pw2, b_pw2, ln_g, ln_b, w_ff1, w_ff2)


EPI_SUB = 256


def _proj_res_ln_kernel(ap_ref, as_ref, w_ref, x_ref, gate_ref, lng_ref, lnb_ref, sc_ref, sh_ref,
                        xo_ref, hmo_ref, *, n_p, tm, sub):
    i = pl.program_id(0)
    for s in range(tm // sub):
        rows = pl.ds(s * sub, sub)
        y = _dot(jnp.where(i < n_p, ap_ref[rows, :], as_ref[rows, :]), w_ref[...])
        xn = _layer_norm(ALPHA * x_ref[rows, :] + gate_ref[...] * y, lng_ref[...], lnb_ref[...])
        xo_ref[rows, :] = xn
        hmo_ref[rows, :] = (xn * (1.0 + sc_ref[...]) + sh_ref[...]).astype(BF16)


def _proj_res_ln(a_ctx, a_lat, w, x, mod, layer, ln_g, ln_b, tm, sub):
    kdim = w.shape[0]
    return pl.pallas_call(
        functools.partial(_proj_res_ln_kernel, n_p=P_TOK // tm, tm=tm, sub=sub),
        grid=(N_TOK // tm,),
        in_specs=_group_specs(tm, kdim) + [
            pl.BlockSpec((kdim, D), lambda i: (0, 0), pipeline_mode=pl.Buffered(1)),
            pl.BlockSpec((tm, D), lambda i: (i, 0)),
            _mod_spec(layer, 2, tm), _const_spec((1, D)), _const_spec((1, D)),
            _mod_spec(layer, 4, tm), _mod_spec(layer, 3, tm)],
        out_specs=[pl.BlockSpec((tm, D), lambda i: (i, 0)),
                   pl.BlockSpec((tm, D), lambda i: (i, 0))],
        out_shape=[jax.ShapeDtypeStruct((N_TOK, D), F32),
                   jax.ShapeDtypeStruct((N_TOK, D), BF16)],
        compiler_params=_cparams(("parallel",)),
        name=f"proj_res_ln_l{layer}",
    )(a_ctx, a_lat, w, x, mod, ln_g, ln_b, mod, mod)


def _ffn_kernel(*refs, nf, n_p, n_side):
    hm_ref, w1_ref, b1_ref, w2_ref, b2_ref, x_ref, gate_ref, lng_ref, lnb_ref = refs[:9]
    last = n_side == 0
    if last:
        yp_ref, ys_ref, acc_ref = refs[9:]
    else:
        sc_ref, sh_ref = refs[9:11]
        side_in = refs[11:11 + n_side]
        xo_ref, hmo_ref = refs[11 + n_side:13 + n_side]
        side_out = refs[13 + n_side:13 + 2 * n_side]
        acc_ref = refs[-1]
        for src, dst in zip(side_in, side_out):
            dst[...] = src[...].astype(BF16)
    i = pl.program_id(0)
    f = pl.program_id(1)
    tm = hm_ref.shape[0]

    h = _dot(hm_ref[...], w1_ref[...]) + b1_ref[...]
    h = jnp.square(jnp.maximum(h, 0.0)).astype(BF16)

    @pl.when(f == 0)
    def _():
        acc_ref[...] = _dot(h, w2_ref[...])

    @pl.when(jnp.logical_and(f > 0, f < nf - 1))
    def _():
        acc_ref[...] += _dot(h, w2_ref[...])

    def finish(store):
        for s in range(tm // EPI_SUB):
            rows = pl.ds(s * EPI_SUB, EPI_SUB)
            y = (acc_ref[rows, :] + _dot(h[s * EPI_SUB:(s + 1) * EPI_SUB], w2_ref[...])
                 + b2_ref[...])
            store(rows, _layer_norm(ALPHA * x_ref[rows, :] + gate_ref[...] * y,
                                    lng_ref[...], lnb_ref[...]))

    def store_ctx(rows, xn):
        yp_ref[rows, :] = xn

    def store_lat(rows, xn):
        ys_ref[rows, :] = xn

    def store_next(rows, xn):
        xo_ref[rows, :] = xn
        hmo_ref[rows, :] = (xn * (1.0 + sc_ref[...]) + sh_ref[...]).astype(BF16)

    is_last_f = f == nf - 1
    if last:
        pl.when(jnp.logical_and(is_last_f, i < n_p))(lambda: finish(store_ctx))
        pl.when(jnp.logical_and(is_last_f, i >= n_p))(lambda: finish(store_lat))
    else:
        pl.when(is_last_f)(lambda: finish(store_next))


def _next_layer_cast_specs(layer, nf):
    step = lambda i, f: i * nf + f
    in_specs, out_specs, out_shape = _ffn_cast_specs(layer + 1, 128, step)
    n_slab, per_slab = 6 * D // DK, 4
    unit = lambda i, f: jnp.minimum(step(i, f), n_slab * per_slab - 1)
    in_specs.append(pl.BlockSpec((None, D // per_slab, DK),
                                 lambda i, f: (0, unit(i, f) % per_slab, unit(i, f) // per_slab)))
    out_specs.append(pl.BlockSpec((None, D // per_slab, DK),
                                  lambda i, f: (unit(i, f) // per_slab, unit(i, f) % per_slab, 0)))
    out_shape.append(jax.ShapeDtypeStruct((n_slab, D, DK), BF16))
    rows_o = 2 * D // 128
    unit_o = lambda i, f: jnp.minimum(step(i, f), 127)
    in_specs.append(pl.BlockSpec((None, rows_o, D), lambda i, f: (0, unit_o(i, f), 0)))
    out_specs.append(pl.BlockSpec((rows_o, D), lambda i, f: (unit_o(i, f), 0)))
    out_shape.append(jax.ShapeDtypeStruct((2 * D, D), BF16))
    return in_specs, out_specs, out_shape


def _ffn(hm, w1_tiles, b1, w2, b2, x, mod, layer, ln_g, ln_b, next_w=None):
    tm, tf = TM, FFN_TF
    nf = D_FF // tf
    n_p = P_TOK // tm
    last = next_w is None
    in_specs = [
        pl.BlockSpec((tm, D), lambda i, f: (i, 0)),
        pl.BlockSpec((None, D, tf), lambda i, f: (f, 0, 0)),
        pl.BlockSpec((1, tf), lambda i, f: (0, f)),
        pl.BlockSpec((tf, D), lambda i, f: (f, 0)),
        _const_spec((1, D)),
        pl.BlockSpec((tm, D), lambda i, f: (i, 0)),
        _mod_spec(layer, 5, tm),
        _const_spec((1, D)), _const_spec((1, D)),
    ]
    args = [hm, w1_tiles, b1, w2, b2, x, mod, ln_g, ln_b]
    if last:
        out_specs = _group_specs(tm, D)
        out_shape = [jax.ShapeDtypeStruct((P_TOK, D), F32), jax.ShapeDtypeStruct((S_TOK, D), F32)]
    else:
        cast_in, cast_out, cast_shape = _next_layer_cast_specs(layer, nf)
        in_specs += [_mod_spec(layer + 1, 1, tm), _mod_spec(layer + 1, 0, tm)] + cast_in
        args += [mod, mod, *next_w]
        out_specs = [pl.BlockSpec((tm, D), lambda i, f: (i, 0))] * 2 + cast_out
        out_shape = [jax.ShapeDtypeStruct((N_TOK, D), F32),
                     jax.ShapeDtypeStruct((N_TOK, D), BF16)] + cast_shape
    return pl.pallas_call(
        functools.partial(_ffn_kernel, nf=nf, n_p=n_p, n_side=0 if last else len(next_w)),
        grid=(N_TOK // tm, nf),
        in_specs=in_specs,
        out_specs=out_specs,
        out_shape=out_shape,
        scratch_shapes=[pltpu.VMEM((tm, D), F32)],
        compiler_params=_cparams(("arbitrary", "arbitrary")),
        name=f"ffn_l{layer}",
    )(*args)


RC = 256


def _scale_rows(a, d):
    return jnp.concatenate(
        [a[:, s * LANES:(s + 1) * LANES] * d for s in range(a.shape[1] // LANES)], axis=1)


def _rope_rows(y, cos, sin):
    half = DK // 2
    outs = []
    for s in range(2):
        x = y[:, s * half:(s + 1) * half]
        rot = pltpu.roll(x, half // 2, axis=1)
        outs.append(x * cos[:, s * half:(s + 1) * half] + rot * sin[:, s * half:(s + 1) * half])
    return jnp.concatenate(outs, axis=1)


def _ret_mixer_kernel(*refs, seq, sps, latent, has_s0, write_state):
    refs = list(refs)
    rate_ref, hm_ref, wq_ref, wk_ref, wv_ref, wg_ref = [refs.pop(0) for _ in range(6)]
    cos_ref, sin_ref = (refs.pop(0), refs.pop(0)) if latent else (None, None)
    s0_ref = refs.pop(0) if has_s0 else None
    o_ref = refs.pop(0)
    so_ref = refs.pop(0) if write_state else None
    mask_ref, din_ref, dout_ref, sb_all, s_all, k_s, v_s = refs
    n = seq // RC
    c_f = float(RC)

    lg = jnp.log1p(-jnp.exp2(rate_ref[...]))
    lgf, lgb = lg[0], lg[1]
    dchunk_f = jnp.exp(lgf * c_f)
    dchunk_b = jnp.exp(lgb * c_f)
    rows_f = lax.broadcasted_iota(jnp.int32, (RC, LANES), 0).astype(F32)
    diff = (lax.broadcasted_iota(jnp.int32, (RC, RC), 0)
            - lax.broadcasted_iota(jnp.int32, (RC, RC), 1)).astype(F32)
    mf = jnp.where(diff >= 0, jnp.exp(lgf[:, :RC] * jnp.maximum(diff, 0.0)), 0.0)
    mb = jnp.where(diff <= 0, jnp.exp(lgb[:, :RC] * jnp.maximum(-diff, 0.0)), 0.0)
    mask_ref[...] = mf + mb
    lf, lb = lgf[:, :LANES], lgb[:, :LANES]
    din_ref[0] = jnp.exp(lf * (rows_f + 1.0))
    din_ref[1] = jnp.exp(lb * (c_f - rows_f))
    dout_ref[0] = jnp.exp(lf * (c_f - 1.0 - rows_f))
    dout_ref[1] = jnp.exp(lb * rows_f)

    def loop(body, unroll):
        if n <= 2:
            for t in range(n):
                body(t)
        else:
            lax.fori_loop(0, n, lambda t, carry: (body(t), carry)[1], 0, unroll=unroll)

    def rope(y, c):
        if not latent:
            return y
        pos = pl.ds(c * RC, RC) if isinstance(c, int) else pl.ds(pl.multiple_of(c * RC, RC), RC)
        return _rope_rows(y, cos_ref[pos, :], sin_ref[pos, :])

    def two(w_ref, x):
        return jnp.concatenate([_dot(x, w_ref[0]), _dot(x, w_ref[1])], axis=1)

    for sq in range(sps):
        sb_ref, s_ref = sb_all.at[sq], s_all.at[sq]

        def chunk_rows(c, sq=sq):
            if isinstance(c, int):
                return pl.ds(sq * seq + c * RC, RC)
            return pl.ds(pl.multiple_of(sq * seq + c * RC, RC), RC)

        def init_state(direction, sq=sq, s_ref=s_ref):
            if has_s0:
                s_ref[...] = s0_ref[sq, direction]
            else:
                s_ref[...] = jnp.zeros((DK, DV), F32)

        def state_update(c, direction, dchunk, chunk_rows=chunk_rows, s_ref=s_ref):
            rows = chunk_rows(c)
            kd = _scale_rows(k_s[rows, :].astype(F32), dout_ref[direction]).astype(BF16)
            upd = lax.dot_general(kd, v_s[rows, :], (((0,), (0,)), ((), ())),
                                  preferred_element_type=F32)
            s_ref[...] = dchunk * s_ref[...] + upd

        def phase0(c, chunk_rows=chunk_rows):
            rows = chunk_rows(c)
            x = hm_ref[rows, :]
            k_s[rows, :] = rope(_dot(x, wk_ref[...]), c).astype(BF16)
            v_s[rows, :] = two(wv_ref, x).astype(BF16)

        loop(phase0, 2)

        init_state(1)

        def phase1(t, sb_ref=sb_ref, s_ref=s_ref, state_update=state_update):
            c = n - 1 - t
            sb_ref[c] = s_ref[...].astype(BF16)
            state_update(c, 1, dchunk_b)

        loop(phase1, 4)
        if write_state:
            so_ref[sq, 1] = s_ref[...]

        init_state(0)

        def phase2(c, sb_ref=sb_ref, s_ref=s_ref, chunk_rows=chunk_rows,
                   state_update=state_update):
            rows = chunk_rows(c)
            x = hm_ref[rows, :]
            qc = rope(_dot(x, wq_ref[...]) * DK ** -0.5, c).astype(BF16)
            gate = _silu(two(wg_ref, x))
            vc = v_s[rows, :]
            scores = lax.dot_general(qc, k_s[rows, :], (((1,), (1,)), ((), ())),
                                     preferred_element_type=F32)
            tot = _dot((scores * mask_ref[...]).astype(BF16), vc)
            if has_s0 or not (isinstance(c, int) and c == 0):
                tot = tot + _scale_rows(_dot(qc, s_ref[...].astype(BF16)), din_ref[0])
            if has_s0 or not (isinstance(c, int) and c == n - 1):
                tot = tot + _scale_rows(_dot(qc, sb_ref[c]), din_ref[1])
            mu = jnp.mean(tot, axis=-1, keepdims=True)
            d = tot - mu
            var = jnp.mean(d * d, axis=-1, keepdims=True)
            nrm = d * lax.rsqrt(var + GN_EPS)
            o_ref[rows, :] = (gate * nrm).astype(BF16)
            state_update(c, 0, dchunk_f)

        loop(phase2, 2)
        if write_state:
            so_ref[sq, 0] = s_ref[...]


def _ret_mixer(rate, hm, w_slabs, tables, s0, *, seq, n_seq, sps, row_block0, write_state):
    has_s0 = s0 is not None
    latent = tables is not None
    koff, voff, goff = D // DK, 2 * D // DV, 4 * D // DV
    rows = sps * seq
    state_block = (sps, None, 2, None, DK, DV)
    in_specs = [
        pl.BlockSpec((2, None, 1, DV), lambda b, h: (0, h, 0, 0)),
        pl.BlockSpec((rows, D), lambda b, h: (row_block0 + b, 0)),
        pl.BlockSpec((None, D, DK), lambda b, h: (h, 0, 0)),
        pl.BlockSpec((None, D, DK), lambda b, h: (koff + h, 0, 0)),
        pl.BlockSpec((2, D, DK), lambda b, h: (voff + h, 0, 0)),
        pl.BlockSpec((2, D, DK), lambda b, h: (goff + h, 0, 0)),
    ]
    args = [rate, hm, w_slabs, w_slabs, w_slabs, w_slabs]
    if latent:
        in_specs += [_const_spec((DEC_SEQ, DK)), _const_spec((DEC_SEQ, DK))]
        args += list(tables)
    if has_s0:
        in_specs.append(pl.BlockSpec(state_block, lambda b, h: (b, 0, 0, h, 0, 0)))
        args.append(s0)
    out_specs = [pl.BlockSpec((rows, DV), lambda b, h: (b, h))]
    out_shape = [jax.ShapeDtypeStruct((n_seq * seq, 2 * D), BF16)]
    if write_state:
        out_specs.append(pl.BlockSpec(state_block, lambda b, h: (b, 0, 0, h, 0, 0)))
        out_shape.append(jax.ShapeDtypeStruct((n_seq, 1, 2, HEADS, DK, DV), F32))
    return pl.pallas_call(
        functools.partial(_ret_mixer_kernel, seq=seq, sps=sps, latent=latent, has_s0=has_s0,
                          write_state=write_state),
        grid=(n_seq // sps, HEADS),
        in_specs=in_specs,
        out_specs=out_specs,
        out_shape=out_shape,
        scratch_shapes=[pltpu.VMEM((RC, RC), F32),
                        pltpu.VMEM((2, RC, LANES), F32),
                        pltpu.VMEM((2, RC, LANES), F32),
                        pltpu.VMEM((sps, seq // RC, DK, DV), BF16),
                        pltpu.VMEM((sps, DK, DV), F32),
                        pltpu.VMEM((rows, DK), BF16),
                        pltpu.VMEM((rows, DV), BF16)],
        compiler_params=_cparams(("parallel", "parallel")),
        name=f"ret_mixer_t{seq}",
    )(*args)


def _rope_tables():
    t = jnp.arange(DEC_SEQ)
    row = (t // GRID_W).astype(F32)
    col = (t % GRID_W).astype(F32)
    quarter = DK // 4
    inv = ROPE_BASE ** (-jnp.arange(quarter, dtype=F32) / quarter)
    ang_r = row[:, None] * inv
    ang_c = col[:, None] * inv
    cos_t = jnp.concatenate([jnp.cos(ang_r)] * 2 + [jnp.cos(ang_c)] * 2, axis=1)
    sin_t = jnp.concatenate([-jnp.sin(ang_r), jnp.sin(ang_r), -jnp.sin(ang_c), jnp.sin(ang_c)], axis=1)
    return cos_t, sin_t


def kernel(x_prompt, x_sample, state_ret, c, c_ctx, w_mod, b_mod, ln1_g, ln1_b, ln2_g, ln2_b,
           w_pw1, b_pw1, w_dw, b_dw, cn_g, cn_b, w_pw2, b_pw2,
           w_ret_in, ret_log2_rate, w_ret_o, w_ff1, b_ff1, w_ff2, b_ff2):
    xp = x_prompt.reshape(P_TOK, D)
    xs = x_sample.reshape(S_TOK, D)
    cond = jnp.concatenate(
        [c_ctx[None, :], c, jnp.zeros((N_COND - 1 - DEC_BATCH, D), F32)], axis=0)

    mod = _modulation(cond, w_mod, b_mod)
    mod = mod.reshape(DEPTH, N_COND, 6, D).transpose(0, 2, 1, 3).reshape(DEPTH * 6 * N_COND, 1, D)

    bf = lambda w: w.astype(BF16)
    row = lambda v: v.reshape(1, -1)

    x, hm, w1_l0, w2_l0 = _conv_mixer(
        xp, xs, mod, bf(w_pw1[0]), row(b_pw1[0]), w_dw[0], row(b_dw[0]), row(cn_g[0]),
        row(cn_b[0]), bf(w_pw2[0]), row(b_pw2[0]), row(ln1_g[0]), row(ln1_b[0]), w_ff1, w_ff2)
    x, hm, w1_l1, w2_l1, w_slabs, w_o = _ffn(
        hm, w1_l0, row(b_ff1[0]), w2_l0, row(b_ff2[0]), x, mod, 0, row(ln2_g[0]), row(ln2_b[0]),
        next_w=(w_ff1, w_ff2, w_ret_in, w_ret_o))

    cos_t, sin_t = _rope_tables()
    rate = jnp.broadcast_to(ret_log2_rate[0][:, :, None, None], (2, HEADS, 1, DV))
    gp, new_state = _ret_mixer(rate, hm, w_slabs, None, None, seq=SEQ, n_seq=BATCH, sps=4,
                               row_block0=0, write_state=True)
    (gs,) = _ret_mixer(rate, hm, w_slabs, (cos_t, sin_t), state_ret, seq=DEC_SEQ,
                       n_seq=DEC_BATCH, sps=1, row_block0=P_TOK // DEC_SEQ, write_state=False)
    x, hm = _proj_res_ln(gp, gs, w_o, x, mod, 1,
                         row(ln1_g[1]), row(ln1_b[1]), tm=256, sub=128)
    yp, ys = _ffn(hm, w1_l1, row(b_ff1[1]), w2_l1, row(b_ff2[1]),
                  x, mod, 1, row(ln2_g[1]), row(ln2_b[1]))

    return (yp.reshape(BATCH, SEQ, D), ys.reshape(DEC_BATCH, DEC_SEQ, D), new_state)
```

```python
import functools

import jax
import jax.numpy as jnp
from jax import lax
from jax.experimental import pallas as pl
from jax.experimental.pallas import tpu as pltpu

F32 = jnp.float32
BF16 = jnp.bfloat16

D = 2048
BATCH = 16
SEQ = 256
DEC_BATCH = 4
DEC_SEQ = 2048
GRID_W = 64
CONV_WIDTH = 31
D_FF = 4 * D
HEADS = 8
DK = D // HEADS
DV = 2 * D // HEADS
ROPE_BASE = 10000.0
LN_EPS = 1e-5
GN_EPS = 1e-5
DEPTH = 2
ALPHA = (2.0 * DEPTH) ** 0.25

P_TOK = BATCH * SEQ
S_TOK = DEC_BATCH * DEC_SEQ
N_TOK = P_TOK + S_TOK
N_COND = 8

LANES = 128
SUBLANES = 8
TM = 512
VMEM_LIMIT = 60 * 1024 * 1024


def _cparams(sem):
    return pltpu.CompilerParams(dimension_semantics=sem, vmem_limit_bytes=VMEM_LIMIT)


def _cond_index(i, tm):
    r = i * tm
    return jnp.where(r < P_TOK, 0, 1 + (r - P_TOK) // DEC_SEQ)


def _mod_spec(layer, part, tm, row_axis=0):
    base = (layer * 6 + part) * N_COND
    return pl.BlockSpec((None, 1, D),
                        lambda *ids: (base + _cond_index(ids[row_axis], tm), 0, 0))


def _const_spec(shape):
    return pl.BlockSpec(shape, lambda *ids: (0,) * len(shape))


def _group_specs(tm, cols, row_axis=0):
    n_p = P_TOK // tm
    ctx = pl.BlockSpec((tm, cols), lambda *ids: (jnp.minimum(ids[row_axis], n_p - 1), 0))
    lat = pl.BlockSpec((tm, cols), lambda *ids: (jnp.maximum(ids[row_axis] - n_p, 0), 0))
    return [ctx, lat]


FFN_TF = 1024


def _ffn_cast_specs(layer, n_units, step_of):
    rows1 = (D_FF // FFN_TF) * D // n_units
    rows2 = D_FF // n_units
    per_tile = D // rows1
    unit = lambda ids: jnp.minimum(step_of(*ids), n_units - 1)
    in_specs = [
        pl.BlockSpec((None, rows1, FFN_TF),
                     lambda *ids: (layer, unit(ids) % per_tile, unit(ids) // per_tile)),
        pl.BlockSpec((None, rows2, D), lambda *ids: (layer, unit(ids), 0)),
    ]
    out_specs = [
        pl.BlockSpec((None, rows1, FFN_TF),
                     lambda *ids: (unit(ids) // per_tile, unit(ids) % per_tile, 0)),
        pl.BlockSpec((rows2, D), lambda *ids: (unit(ids), 0)),
    ]
    out_shape = [jax.ShapeDtypeStruct((D_FF // FFN_TF, D, FFN_TF), BF16),
                 jax.ShapeDtypeStruct((D_FF, D), BF16)]
    return in_specs, out_specs, out_shape


def _layer_norm(r, g, b):
    mu = jnp.mean(r, axis=-1, keepdims=True)
    d = r - mu
    var = jnp.mean(d * d, axis=-1, keepdims=True)
    return d * lax.rsqrt(var + LN_EPS) * g + b


def _silu(x):
    return x * jax.nn.sigmoid(x)


def _dot(a, b):
    return jnp.dot(a, b, preferred_element_type=F32)


def _mod_kernel(cond_ref, w_ref, b_ref, o_ref):
    s = _silu(cond_ref[...]).astype(BF16)
    o_ref[...] = _dot(s, w_ref[...].astype(BF16)) + b_ref[...]


def _modulation(cond, w_mod, b_mod):
    tn = 1024
    return pl.pallas_call(
        _mod_kernel,
        grid=(DEPTH, 6 * D // tn),
        in_specs=[
            pl.BlockSpec((N_COND, D), lambda l, j: (0, 0)),
            pl.BlockSpec((None, D, tn), lambda l, j: (l, 0, j)),
            pl.BlockSpec((None, 1, tn), lambda l, j: (l, 0, j)),
        ],
        out_specs=pl.BlockSpec((None, N_COND, tn), lambda l, j: (l, 0, j)),
        out_shape=jax.ShapeDtypeStruct((DEPTH, N_COND, 6 * D), F32),
        compiler_params=_cparams(("parallel", "parallel")),
        name="modulation",
    )(cond, w_mod, b_mod.reshape(DEPTH, 1, 6 * D))


CONV_TB = 256
CONV_HALO = 16
CONV_WIN = CONV_TB + 2 * CONV_HALO
CONV_CW = 256
CONV_RB = 32
CONV_SUB = 128


def _dw_taps(win_ref, sh_ref, wb_ref, y_ref, c0):
    shift = CONV_HALO - CONV_WIDTH // 2
    cols = slice(c0, c0 + CONV_CW)
    w0 = win_ref[:, cols]
    for s in range(1, SUBLANES):
        sh_ref[s - 1] = pltpu.roll(w0, CONV_WIN - s, axis=0)
    groups = CONV_RB // SUBLANES
    for r in range(0, CONV_TB, CONV_RB):
        accs = [jnp.zeros((SUBLANES, CONV_CW), F32)] * groups
        for k in range(CONV_WIDTH):
            off = shift + k
            s, base = off % SUBLANES, r + off - off % SUBLANES
            if s == 0:
                src = win_ref[base:base + CONV_RB, cols]
            else:
                src = sh_ref[s - 1, base:base + CONV_RB, :]
            wk = wb_ref[k, :, cols]
            accs = [a + src[g * SUBLANES:(g + 1) * SUBLANES] * wk for g, a in enumerate(accs)]
        y_ref[r:r + CONV_RB, cols] = jnp.concatenate(accs, axis=0)


def _conv_mixer_kernel(xp_ref, xs_ref, prev_ref, next_ref, sc1_ref, sh1_ref, gate_ref,
                       sc2_ref, sh2_ref, w1_ref, b1_ref, wdw_ref, bdw_ref, cng_ref, cnb_ref,
                       w2_ref, b2_ref, lng_ref, lnb_ref, f1_ref, f2_ref,
                       xo_ref, hmo_ref, f1o_ref, f2o_ref,
                       hm_ref, win_ref, sh_ref, wb_ref, y_ref):
    i = pl.program_id(0)
    f1o_ref[...] = f1_ref[...].astype(BF16)
    f2o_ref[...] = f2_ref[...].astype(BF16)

    @pl.when(i == 0)
    def _():
        for k in range(CONV_WIDTH):
            wb_ref[k] = jnp.broadcast_to(wdw_ref[pl.ds(k, 1), :], (SUBLANES, D))

    n_p = P_TOK // CONV_TB
    per_seq = DEC_SEQ // CONV_TB
    pos = (i - n_p) % per_seq
    latent = i >= n_p
    has_prev = jnp.logical_and(latent, pos != 0)
    has_next = jnp.logical_and(latent, pos != per_seq - 1)
    lo, hi = CONV_HALO, CONV_HALO + CONV_TB

    def modulate(x):
        return (x * (1.0 + sc1_ref[...]) + sh1_ref[...]).astype(BF16)

    hm_ref[0:lo, :] = modulate(prev_ref[...])
    hm_ref[lo:hi, :] = modulate(jnp.where(latent, xs_ref[...], xp_ref[...]))
    hm_ref[hi:, :] = modulate(next_ref[...])

    for c0 in range(0, D, CONV_CW):
        hm = hm_ref[...]
        a = _dot(hm, w1_ref[:, c0:c0 + CONV_CW]) + b1_ref[:, c0:c0 + CONV_CW]
        g = _dot(hm, w1_ref[:, D + c0:D + c0 + CONV_CW]) + b1_ref[:, D + c0:D + c0 + CONV_CW]
        u = a * jax.nn.sigmoid(g)
        win_ref[0:lo, c0:c0 + CONV_CW] = jnp.where(has_prev, u[0:lo], 0.0)
        win_ref[lo:hi, c0:c0 + CONV_CW] = u[lo:hi]
        win_ref[hi:, c0:c0 + CONV_CW] = jnp.where(has_next, u[hi:], 0.0)
        _dw_taps(win_ref, sh_ref, wb_ref, y_ref, c0)

    v = _silu(_layer_norm(y_ref[...] + bdw_ref[...], cng_ref[...], cnb_ref[...])).astype(BF16)
    for s in range(CONV_TB // CONV_SUB):
        rows = slice(s * CONV_SUB, (s + 1) * CONV_SUB)
        y2 = _dot(v[rows], w2_ref[...]) + b2_ref[...]
        x = jnp.where(latent, xs_ref[rows, :], xp_ref[rows, :])
        xn = _layer_norm(ALPHA * x + gate_ref[...] * y2, lng_ref[...], lnb_ref[...])
        xo_ref[rows, :] = xn
        hmo_ref[rows, :] = (xn * (1.0 + sc2_ref[...]) + sh2_ref[...]).astype(BF16)


def _conv_mixer(xp, xs, mod, w_pw1, b_pw1, w_dw, b_dw, cn_g, cn_b, w_pw2, b_pw2, ln_g, ln_b,
                w_ff1, w_ff2):
    tb = CONV_TB
    n_p = P_TOK // tb
    hb = tb // CONV_HALO
    last_halo = S_TOK // CONV_HALO - 1
    vec = _const_spec((1, D))
    resident = lambda shape: pl.BlockSpec(shape, lambda i: (0, 0), pipeline_mode=pl.Buffered(1))
    cast_in, cast_out, cast_shape = _ffn_cast_specs(0, 32, lambda i: i)
    return pl.pallas_call(
        _conv_mixer_kernel,
        grid=(N_TOK // tb,),
        in_specs=_group_specs(tb, D) + [
            pl.BlockSpec((CONV_HALO, D), lambda i: (jnp.clip((i - n_p) * hb - 1, 0, last_halo), 0)),
            pl.BlockSpec((CONV_HALO, D), lambda i: (jnp.clip((i - n_p + 1) * hb, 0, last_halo), 0)),
            _mod_spec(0, 1, tb), _mod_spec(0, 0, tb), _mod_spec(0, 2, tb),
            _mod_spec(0, 4, tb), _mod_spec(0, 3, tb),
            resident((D, 2 * D)), _const_spec((1, 2 * D)),
            _const_spec((CONV_WIDTH, D)), vec, vec, vec,
            resident((D, D)), vec, vec, vec,
        ] + cast_in,
        out_specs=[pl.BlockSpec((tb, D), lambda i: (i, 0)),
                   pl.BlockSpec((tb, D), lambda i: (i, 0))] + cast_out,
        out_shape=[jax.ShapeDtypeStruct((N_TOK, D), F32),
                   jax.ShapeDtypeStruct((N_TOK, D), BF16)] + cast_shape,
        scratch_shapes=[pltpu.VMEM((CONV_WIN, D), BF16),
                        pltpu.VMEM((CONV_WIN, D), F32),
                        pltpu.VMEM((SUBLANES - 1, CONV_WIN, CONV_CW), F32),
                        pltpu.VMEM((CONV_WIDTH, SUBLANES, D), F32),
                        pltpu.VMEM((CONV_TB, D), F32)],
        compiler_params=_cparams(("arbitrary",)),
        name="conv_mixer",
    )(xp, xs, xs, xs, mod, mod, mod, mod, mod, w_pw1, b_pw1, w_dw, b_dw, cn_g, cn_b,
      w_pw2, b_pw2, ln_g, ln_b, w_ff1, w_ff2)


EPI_SUB = 256


def _proj_res_ln_kernel(ap_ref, as_ref, w_ref, x_ref, gate_ref, lng_ref, lnb_ref, sc_ref, sh_ref,
                        xo_ref, hmo_ref, *, n_p, tm, sub):
    i = pl.program_id(0)
    for s in range(tm // sub):
        rows = pl.ds(s * sub, sub)
        y = _dot(jnp.where(i < n_p, ap_ref[rows, :], as_ref[rows, :]), w_ref[...])
        xn = _layer_norm(ALPHA * x_ref[rows, :] + gate_ref[...] * y, lng_ref[...], lnb_ref[...])
        xo_ref[rows, :] = xn
        hmo_ref[rows, :] = (xn * (1.0 + sc_ref[...]) + sh_ref[...]).astype(BF16)


def _proj_res_ln(a_ctx, a_lat, w, x, mod, layer, ln_g, ln_b, tm, sub):
    kdim = w.shape[0]
    return pl.pallas_call(
        functools.partial(_proj_res_ln_kernel, n_p=P_TOK // tm, tm=tm, sub=sub),
        grid=(N_TOK // tm,),
        in_specs=_group_specs(tm, kdim) + [
            pl.BlockSpec((kdim, D), lambda i: (0, 0), pipeline_mode=pl.Buffered(1)),
            pl.BlockSpec((tm, D), lambda i: (i, 0)),
            _mod_spec(layer, 2, tm), _const_spec((1, D)), _const_spec((1, D)),
            _mod_spec(layer, 4, tm), _mod_spec(layer, 3, tm)],
        out_specs=[pl.BlockSpec((tm, D), lambda i: (i, 0)),
                   pl.BlockSpec((tm, D), lambda i: (i, 0))],
        out_shape=[jax.ShapeDtypeStruct((N_TOK, D), F32),
                   jax.ShapeDtypeStruct((N_TOK, D), BF16)],
        compiler_params=_cparams(("parallel",)),
        name=f"proj_res_ln_l{layer}",
    )(a_ctx, a_lat, w, x, mod, ln_g, ln_b, mod, mod)


def _ffn_kernel(*refs, nf, n_p, n_side):
    hm_ref, w1_ref, b1_ref, w2_ref, b2_ref, x_ref, gate_ref, lng_ref, lnb_ref = refs[:9]
    last = n_side == 0
    if last:
        yp_ref, ys_ref, acc_ref = refs[9:]
    else:
        sc_ref, sh_ref = refs[9:11]
        side_in = refs[11:11 + n_side]
        xo_ref, hmo_ref = refs[11 + n_side:13 + n_side]
        side_out = refs[13 + n_side:13 + 2 * n_side]
        acc_ref = refs[-1]
        for src, dst in zip(side_in, side_out):
            dst[...] = src[...].astype(BF16)
    i = pl.program_id(0)
    f = pl.program_id(1)
    tm = hm_ref.shape[0]

    h = _dot(hm_ref[...], w1_ref[...]) + b1_ref[...]
    h = jnp.square(jnp.maximum(h, 0.0)).astype(BF16)

    @pl.when(f == 0)
    def _():
        acc_ref[...] = _dot(h, w2_ref[...])

    @pl.when(jnp.logical_and(f > 0, f < nf - 1))
    def _():
        acc_ref[...] += _dot(h, w2_ref[...])

    def finish(store):
        for s in range(tm // EPI_SUB):
            rows = pl.ds(s * EPI_SUB, EPI_SUB)
            y = (acc_ref[rows, :] + _dot(h[s * EPI_SUB:(s + 1) * EPI_SUB], w2_ref[...])
                 + b2_ref[...])
            store(rows, _layer_norm(ALPHA * x_ref[rows, :] + gate_ref[...] * y,
                                    lng_ref[...], lnb_ref[...]))

    def store_ctx(rows, xn):
        yp_ref[rows, :] = xn

    def store_lat(rows, xn):
        ys_ref[rows, :] = xn

    def store_next(rows, xn):
        xo_ref[rows, :] = xn
        hmo_ref[rows, :] = (xn * (1.0 + sc_ref[...]) + sh_ref[...]).astype(BF16)

    is_last_f = f == nf - 1
    if last:
        pl.when(jnp.logical_and(is_last_f, i < n_p))(lambda: finish(store_ctx))
        pl.when(jnp.logical_and(is_last_f, i >= n_p))(lambda: finish(store_lat))
    else:
        pl.when(is_last_f)(lambda: finish(store_next))


def _next_layer_cast_specs(layer, nf):
    step = lambda i, f: i * nf + f
    n_slab, per_slab = 6 * D // DK, 4
    unit = lambda i, f: jnp.minimum(step(i, f), n_slab * per_slab - 1)
    in_specs = [pl.BlockSpec((None, D // per_slab, DK),
                             lambda i, f: (0, unit(i, f) % per_slab, unit(i, f) // per_slab))]
    out_specs = [pl.BlockSpec((None, D // per_slab, DK),
                              lambda i, f: (unit(i, f) // per_slab, unit(i, f) % per_slab, 0))]
    out_shape = [jax.ShapeDtypeStruct((n_slab, D, DK), BF16)]
    return in_specs, out_specs, out_shape


def _ffn(hm, w1_tiles, b1, w2, b2, x, mod, layer, ln_g, ln_b, next_w=None):
    tm, tf = TM, FFN_TF
    nf = D_FF // tf
    n_p = P_TOK // tm
    last = next_w is None
    in_specs = [
        pl.BlockSpec((tm, D), lambda i, f: (i, 0)),
        pl.BlockSpec((None, D, tf), lambda i, f: (f, 0, 0)),
        pl.BlockSpec((1, tf), lambda i, f: (0, f)),
        pl.BlockSpec((tf, D), lambda i, f: (f, 0)),
        _const_spec((1, D)),
        pl.BlockSpec((tm, D), lambda i, f: (i, 0)),
        _mod_spec(layer, 5, tm),
        _const_spec((1, D)), _const_spec((1, D)),
    ]
    args = [hm, w1_tiles, b1, w2, b2, x, mod, ln_g, ln_b]
    if last:
        out_specs = _group_specs(tm, D)
        out_shape = [jax.ShapeDtypeStruct((P_TOK, D), F32), jax.ShapeDtypeStruct((S_TOK, D), F32)]
    else:
        cast_in, cast_out, cast_shape = _next_layer_cast_specs(layer, nf)
        in_specs += [_mod_spec(layer + 1, 1, tm), _mod_spec(layer + 1, 0, tm)] + cast_in
        args += [mod, mod, *next_w]
        out_specs = [pl.BlockSpec((tm, D), lambda i, f: (i, 0))] * 2 + cast_out
        out_shape = [jax.ShapeDtypeStruct((N_TOK, D), F32),
                     jax.ShapeDtypeStruct((N_TOK, D), BF16)] + cast_shape
    return pl.pallas_call(
        functools.partial(_ffn_kernel, nf=nf, n_p=n_p, n_side=0 if last else len(next_w)),
        grid=(N_TOK // tm, nf),
        in_specs=in_specs,
        out_specs=out_specs,
        out_shape=out_shape,
        scratch_shapes=[pltpu.VMEM((tm, D), F32)],
        compiler_params=_cparams(("arbitrary", "arbitrary")),
        name=f"ffn_l{layer}",
    )(*args)


RC = 256


def _scale_rows(a, d):
    return jnp.concatenate(
        [a[:, s * LANES:(s + 1) * LANES] * d for s in range(a.shape[1] // LANES)], axis=1)


def _rope_rows(y, cos, sin):
    half = DK // 2
    outs = []
    for s in range(2):
        x = y[:, s * half:(s + 1) * half]
        rot = pltpu.roll(x, half // 2, axis=1)
        outs.append(x * cos[:, s * half:(s + 1) * half] + rot * sin[:, s * half:(s + 1) * half])
    return jnp.concatenate(outs, axis=1)


def _ret_mixer_kernel(*refs, seq, sps, latent, has_s0, write_state, n_side):
    refs = list(refs)
    rate_ref, hm_ref, wq_ref, wk_ref, wv_ref, wg_ref = [refs.pop(0) for _ in range(6)]
    cos_ref, sin_ref = (refs.pop(0), refs.pop(0)) if latent else (None, None)
    s0_ref = refs.pop(0) if has_s0 else None
    side_in = [refs.pop(0) for _ in range(n_side)]
    o_ref = refs.pop(0)
    so_ref = refs.pop(0) if write_state else None
    side_out = [refs.pop(0) for _ in range(n_side)]
    mask_ref, din_ref, dout_ref, sb_all, s_all, k_s, v_s = refs
    for src, dst in zip(side_in, side_out):
        dst[...] = src[...].astype(BF16)
    n = seq // RC
    c_f = float(RC)

    lg = jnp.log1p(-jnp.exp2(rate_ref[...]))
    lgf, lgb = lg[0], lg[1]
    dchunk_f = jnp.exp(lgf * c_f)
    dchunk_b = jnp.exp(lgb * c_f)
    rows_f = lax.broadcasted_iota(jnp.int32, (RC, LANES), 0).astype(F32)
    diff = (lax.broadcasted_iota(jnp.int32, (RC, RC), 0)
            - lax.broadcasted_iota(jnp.int32, (RC, RC), 1)).astype(F32)
    mf = jnp.where(diff >= 0, jnp.exp(lgf[:, :RC] * jnp.maximum(diff, 0.0)), 0.0)
    mb = jnp.where(diff <= 0, jnp.exp(lgb[:, :RC] * jnp.maximum(-diff, 0.0)), 0.0)
    mask_ref[...] = mf + mb
    lf, lb = lgf[:, :LANES], lgb[:, :LANES]
    din_ref[0] = jnp.exp(lf * (rows_f + 1.0))
    din_ref[1] = jnp.exp(lb * (c_f - rows_f))
    dout_ref[0] = jnp.exp(lf * (c_f - 1.0 - rows_f))
    dout_ref[1] = jnp.exp(lb * rows_f)

    def loop(body, unroll):
        if n <= 2:
            for t in range(n):
                body(t)
        else:
            lax.fori_loop(0, n, lambda t, carry: (body(t), carry)[1], 0, unroll=unroll)

    def rope(y, c):
        if not latent:
            return y
        pos = pl.ds(c * RC, RC) if isinstance(c, int) else pl.ds(pl.multiple_of(c * RC, RC), RC)
        return _rope_rows(y, cos_ref[pos, :], sin_ref[pos, :])

    def two(w_ref, x):
        return jnp.concatenate([_dot(x, w_ref[0]), _dot(x, w_ref[1])], axis=1)

    for sq in range(sps):
        sb_ref, s_ref = sb_all.at[sq], s_all.at[sq]

        def chunk_rows(c, sq=sq):
            if isinstance(c, int):
                return pl.ds(sq * seq + c * RC, RC)
            return pl.ds(pl.multiple_of(sq * seq + c * RC, RC), RC)

        def init_state(direction, sq=sq, s_ref=s_ref):
            if has_s0:
                s_ref[...] = s0_ref[sq, direction]
            else:
                s_ref[...] = jnp.zeros((DK, DV), F32)

        def state_update(c, direction, dchunk, chunk_rows=chunk_rows, s_ref=s_ref):
            rows = chunk_rows(c)
            kd = _scale_rows(k_s[rows, :].astype(F32), dout_ref[direction]).astype(BF16)
            upd = lax.dot_general(kd, v_s[rows, :], (((0,), (0,)), ((), ())),
                                  preferred_element_type=F32)
            s_ref[...] = dchunk * s_ref[...] + upd

        def phase0(c, chunk_rows=chunk_rows):
            rows = chunk_rows(c)
            x = hm_ref[rows, :]
            k_s[rows, :] = rope(_dot(x, wk_ref[...]), c).astype(BF16)
            v_s[rows, :] = two(wv_ref, x).astype(BF16)

        loop(phase0, 2)

        init_state(1)

        def phase1(t, sb_ref=sb_ref, s_ref=s_ref, state_update=state_update):
            c = n - 1 - t
            sb_ref[c] = s_ref[...].astype(BF16)
            state_update(c, 1, dchunk_b)

        loop(phase1, 4)
        if write_state:
            so_ref[sq, 1] = s_ref[...]

        init_state(0)

        def phase2(c, sb_ref=sb_ref, s_ref=s_ref, chunk_rows=chunk_rows,
                   state_update=state_update):
            rows = chunk_rows(c)
            x = hm_ref[rows, :]
            qc = rope(_dot(x, wq_ref[...]) * DK ** -0.5, c).astype(BF16)
            gate = _silu(two(wg_ref, x))
            vc = v_s[rows, :]
            scores = lax.dot_general(qc, k_s[rows, :], (((1,), (1,)), ((), ())),
                                     preferred_element_type=F32)
            tot = _dot((scores * mask_ref[...]).astype(BF16), vc)
            if has_s0 or not (isinstance(c, int) and c == 0):
                tot = tot + _scale_rows(_dot(qc, s_ref[...].astype(BF16)), din_ref[0])
            if has_s0 or not (isinstance(c, int) and c == n - 1):
                tot = tot + _scale_rows(_dot(qc, sb_ref[c]), din_ref[1])
            mu = jnp.mean(tot, axis=-1, keepdims=True)
            d = tot - mu
            var = jnp.mean(d * d, axis=-1, keepdims=True)
            nrm = d * lax.rsqrt(var + GN_EPS)
            o_ref[rows, :] = (gate * nrm).astype(BF16)
            state_update(c, 0, dchunk_f)

        loop(phase2, 2)
        if write_state:
            so_ref[sq, 0] = s_ref[...]


def _ret_mixer(rate, hm, w_slabs, tables, s0, *, seq, n_seq, sps, row_block0, write_state,
               later_w=None):
    has_s0 = s0 is not None
    latent = tables is not None
    koff, voff, goff = D // DK, 2 * D // DV, 4 * D // DV
    rows = sps * seq
    state_block = (sps, None, 2, None, DK, DV)
    in_specs = [
        pl.BlockSpec((2, None, 1, DV), lambda b, h: (0, h, 0, 0)),
        pl.BlockSpec((rows, D), lambda b, h: (row_block0 + b, 0)),
        pl.BlockSpec((None, D, DK), lambda b, h: (h, 0, 0)),
        pl.BlockSpec((None, D, DK), lambda b, h: (koff + h, 0, 0)),
        pl.BlockSpec((2, D, DK), lambda b, h: (voff + h, 0, 0)),
        pl.BlockSpec((2, D, DK), lambda b, h: (goff + h, 0, 0)),
    ]
    args = [rate, hm, w_slabs, w_slabs, w_slabs, w_slabs]
    if latent:
        in_specs += [_const_spec((DEC_SEQ, DK)), _const_spec((DEC_SEQ, DK))]
        args += list(tables)
    if has_s0:
        in_specs.append(pl.BlockSpec(state_block, lambda b, h: (b, 0, 0, h, 0, 0)))
        args.append(s0)
    out_specs = [pl.BlockSpec((rows, DV), lambda b, h: (b, h))]
    out_shape = [jax.ShapeDtypeStruct((n_seq * seq, 2 * D), BF16)]
    if write_state:
        out_specs.append(pl.BlockSpec(state_block, lambda b, h: (b, 0, 0, h, 0, 0)))
        out_shape.append(jax.ShapeDtypeStruct((n_seq, 1, 2, HEADS, DK, DV), F32))
    n_side = 0
    if later_w is not None:
        n_steps = (n_seq // sps) * HEADS
        step = lambda b, h: b * HEADS + h
        cast_in, cast_out, cast_shape = _ffn_cast_specs(1, n_steps, step)
        rows_o = 2 * D // n_steps
        cast_in.append(pl.BlockSpec((None, rows_o, D), lambda b, h: (0, step(b, h), 0)))
        cast_out.append(pl.BlockSpec((rows_o, D), lambda b, h: (step(b, h), 0)))
        cast_shape.append(jax.ShapeDtypeStruct((2 * D, D), BF16))
        in_specs += cast_in
        args += list(later_w)
        out_specs += cast_out
        out_shape += cast_shape
        n_side = len(later_w)
    return pl.pallas_call(
        functools.partial(_ret_mixer_kernel, seq=seq, sps=sps, latent=latent, has_s0=has_s0,
                          write_state=write_state, n_side=n_side),
        grid=(n_seq // sps, HEADS),
        in_specs=in_specs,
        out_specs=out_specs,
        out_shape=out_shape,
        scratch_shapes=[pltpu.VMEM((RC, RC), F32),
                        pltpu.VMEM((2, RC, LANES), F32),
                        pltpu.VMEM((2, RC, LANES), F32),
                        pltpu.VMEM((sps, seq // RC, DK, DV), BF16),
                        pltpu.VMEM((sps, DK, DV), F32),
                        pltpu.VMEM((rows, DK), BF16),
                        pltpu.VMEM((rows, DV), BF16)],
        compiler_params=_cparams(("parallel", "parallel")),
        name=f"ret_mixer_t{seq}",
    )(*args)


def _rope_tables():
    t = jnp.arange(DEC_SEQ)
    row = (t // GRID_W).astype(F32)
    col = (t % GRID_W).astype(F32)
    quarter = DK // 4
    inv = ROPE_BASE ** (-jnp.arange(quarter, dtype=F32) / quarter)
    ang_r = row[:, None] * inv
    ang_c = col[:, None] * inv
    cos_t = jnp.concatenate([jnp.cos(ang_r)] * 2 + [jnp.cos(ang_c)] * 2, axis=1)
    sin_t = jnp.concatenate([-jnp.sin(ang_r), jnp.sin(ang_r), -jnp.sin(ang_c), jnp.sin(ang_c)], axis=1)
    return cos_t, sin_t


def kernel(x_prompt, x_sample, state_ret, c, c_ctx, w_mod, b_mod, ln1_g, ln1_b, ln2_g, ln2_b,
           w_pw1, b_pw1, w_dw, b_dw, cn_g, cn_b, w_pw2, b_pw2,
           w_ret_in, ret_log2_rate, w_ret_o, w_ff1, b_ff1, w_ff2, b_ff2):
    xp = x_prompt.reshape(P_TOK, D)
    xs = x_sample.reshape(S_TOK, D)
    cond = jnp.concatenate(
        [c_ctx[None, :], c, jnp.zeros((N_COND - 1 - DEC_BATCH, D), F32)], axis=0)

    mod = _modulation(cond, w_mod, b_mod)
    mod = mod.reshape(DEPTH, N_COND, 6, D).transpose(0, 2, 1, 3).reshape(DEPTH * 6 * N_COND, 1, D)

    bf = lambda w: w.astype(BF16)
    row = lambda v: v.reshape(1, -1)

    x, hm, w1_l0, w2_l0 = _conv_mixer(
        xp, xs, mod, bf(w_pw1[0]), row(b_pw1[0]), w_dw[0], row(b_dw[0]), row(cn_g[0]),
        row(cn_b[0]), bf(w_pw2[0]), row(b_pw2[0]), row(ln1_g[0]), row(ln1_b[0]), w_ff1, w_ff2)
    x, hm, w_slabs = _ffn(
        hm, w1_l0, row(b_ff1[0]), w2_l0, row(b_ff2[0]), x, mod, 0, row(ln2_g[0]), row(ln2_b[0]),
        next_w=(w_ret_in,))

    cos_t, sin_t = _rope_tables()
    rate = jnp.broadcast_to(ret_log2_rate[0][:, :, None, None], (2, HEADS, 1, DV))
    gp, new_state, w1_l1, w2_l1, w_o = _ret_mixer(
        rate, hm, w_slabs, None, None, seq=SEQ, n_seq=BATCH, sps=4, row_block0=0,
        write_state=True, later_w=(w_ff1, w_ff2, w_ret_o))
    (gs,) = _ret_mixer(rate, hm, w_slabs, (cos_t, sin_t), state_ret, seq=DEC_SEQ,
                       n_seq=DEC_BATCH, sps=1, row_block0=P_TOK // DEC_SEQ, write_state=False)
    x, hm = _proj_res_ln(gp, gs, w_o, x, mod, 1,
                         row(ln1_g[1]), row(ln1_b[1]), tm=256, sub=128)
    yp, ys = _ffn(hm, w1_l1, row(b_ff1[1]), w2_l1, row(b_ff2[1]),
                  x, mod, 1, row(ln2_g[1]), row(ln2_b[1]))

    return (yp.reshape(BATCH, SEQ, D), ys.reshape(DEC_BATCH, DEC_SEQ, D), new_state)
```
